```python
import math
import jax, jax.numpy as jnp
from jax import lax
import numpy as np

D_MODEL = 1024
BATCH = 4
SEQ = 4096
DEPTH = 4
DEC_BATCH = 8
DEC_SEQ = 16
PAST_LEN = 2048

CHUNK = 64
Q_BLOCK = 128
EPS = 1e-6
NEG_INF = -1e30

MLA_HEADS = 8
MLA_Q_RANK = 256
MLA_KV_RANK = 128
MLA_NOPE = 64
MLA_ROPE = 32
MLA_V = 64
MLA_WIDTH = MLA_HEADS * MLA_V
ROPE_BASE = 10000.0

DIFF_HEADS = 8
DIFF_D = 32
DIFF_V = 2 * DIFF_D
DIFF_WIDTH = DIFF_HEADS * DIFF_V

LRU_WIDTH = 512
LRU_BLOCKS = 8
LRU_BW = LRU_WIDTH // LRU_BLOCKS
CONV_W = 4
LRU_C = 8.0

N_BRANCH = 3
SPLITS = (MLA_Q_RANK, MLA_KV_RANK, MLA_ROPE, MLA_WIDTH,
          DIFF_WIDTH, DIFF_WIDTH, DIFF_WIDTH, DIFF_WIDTH,
          LRU_WIDTH, LRU_WIDTH, N_BRANCH * D_MODEL)
N_IN = sum(SPLITS)

kernel_name = 'hybrid_mla_diff_rglru_stream_step'


def rms_norm(x, g):
    xf = x.astype(jnp.float32)
    y = xf * lax.rsqrt(jnp.mean(xf * xf, axis=-1, keepdims=True) + EPS)
    return (y * g.astype(jnp.float32)).astype(x.dtype)


def rope(x, pos):
    half = x.shape[-1] // 2
    inv = ROPE_BASE ** (-jnp.arange(half, dtype=jnp.float32) / half)
    ang = pos.astype(jnp.float32)[:, None] * inv[None, :]
    cos = jnp.cos(ang)[:, None, :]
    sin = jnp.sin(ang)[:, None, :]
    x1 = x[..., :half].astype(jnp.float32)
    x2 = x[..., half:].astype(jnp.float32)
    return jnp.concatenate([x1 * cos - x2 * sin, x2 * cos + x1 * sin], axis=-1).astype(x.dtype)


def alibi_slopes(n):
    return 2.0 ** (-8.0 * jnp.arange(1, n + 1, dtype=jnp.float32) / n)


def chunk_visible(q_pos, k_pos):
    return (k_pos[None, :] // CHUNK) <= (q_pos[:, None] // CHUNK)


def map_query_blocks(fn, q_arrays, q_pos):
    T = q_pos.shape[0]
    if T <= Q_BLOCK:
        return fn(q_arrays, q_pos)
    nb = T // Q_BLOCK
    blocks = tuple(jnp.moveaxis(a.reshape(a.shape[0], nb, Q_BLOCK, *a.shape[2:]), 1, 0) for a in q_arrays)
    out = lax.map(lambda args: fn(args[0], args[1]), (blocks, q_pos.reshape(nb, Q_BLOCK)))
    out = jnp.moveaxis(out, 0, 1)
    return out.reshape(out.shape[0], T, *out.shape[3:])


def rg_lru(xb, z, q_pos, conv0, h0, lp):
    B, T, W = xb.shape
    conv_in = jnp.concatenate([conv0, xb], axis=1)
    conv_new = conv_in[:, -(CONV_W - 1):]
    xc = lp['lru_conv_b'] + sum(conv_in[:, k:k + T] * lp['lru_conv_w'][k] for k in range(CONV_W))
    xcb = xc.reshape(B, T, LRU_BLOCKS, LRU_BW)
    r = jax.nn.sigmoid(jnp.einsum('btni,nio->btno', xcb, lp['lru_w_a']).reshape(B, T, W).astype(jnp.float32)
                       + lp['lru_b_a'].astype(jnp.float32))
    i = jax.nn.sigmoid(jnp.einsum('btni,nio->btno', xcb, lp['lru_w_x']).reshape(B, T, W).astype(jnp.float32)
                       + lp['lru_b_x'].astype(jnp.float32))
    log_a = -LRU_C * r * jax.nn.softplus(-lp['lru_lambda'].astype(jnp.float32))
    a = jnp.exp(log_a)
    mult = jnp.where((q_pos == 0)[None, :, None], 1.0, jnp.sqrt(-jnp.expm1(2.0 * log_a)))
    b = mult * i * xc.astype(jnp.float32)

    def combine(left, right):
        a1, b1 = left
        a2, b2 = right
        return a1 * a2, a2 * b1 + b2

    a_cum, b_cum = lax.associative_scan(combine, (a, b), axis=1)
    h = a_cum * h0.astype(jnp.float32)[:, None, :] + b_cum
    out = h.astype(xb.dtype) * jax.nn.silu(z)
    return out, h[:, -1].astype(xb.dtype), conv_new


def _layer(x, past, lp, layer_idx):
    ckv_past, krope_past, dk_past, dv_past, h0, conv0 = past
    B, T, _ = x.shape
    P = ckv_past.shape[1]
    q_pos = P + jnp.arange(T, dtype=jnp.int32)
    k_pos = jnp.arange(P + T, dtype=jnp.int32)

    xn = rms_norm(x, lp['norm'])
    proj = jnp.einsum('btd,dn->btn', xn, lp['w_in'])
    pieces = []
    off = 0
    for w in SPLITS:
        pieces.append(proj[..., off:off + w])
        off += w
    c_q, c_kv, k_r, z_mla, q_d, k_d, v_d, z_diff, x_lru, z_lru, gate_logits = pieces

    c_q = rms_norm(c_q, lp['mla_q_norm'])
    q = jnp.einsum('btr,rn->btn', c_q, lp['mla_w_uq']).reshape(B, T, MLA_HEADS, MLA_NOPE + MLA_ROPE)
    q_nope = q[..., :MLA_NOPE]
    q_rope = rope(q[..., MLA_NOPE:], q_pos)
    c_kv = rms_norm(c_kv, lp['mla_kv_norm'])
    k_r = rope(k_r[:, :, None, :], q_pos)[:, :, 0, :]
    ckv_all = jnp.concatenate([ckv_past, c_kv], axis=1)
    krope_all = jnp.concatenate([krope_past, k_r], axis=1)
    q_lat = jnp.einsum('bthd,chd->bthc', q_nope, lp['mla_w_uk'])
    mla_scale = (MLA_NOPE + MLA_ROPE) ** -0.5

    def mla_block(qs, qp):
        ql, qr = qs
        s = jnp.einsum('bthc,bsc->bhts', ql, ckv_all) + jnp.einsum('bthr,bsr->bhts', qr, krope_all)
        s = jnp.where(chunk_visible(qp, k_pos)[None, None], s.astype(jnp.float32) * mla_scale, NEG_INF)
        p = jax.nn.softmax(s, axis=-1).astype(ckv_all.dtype)
        return jnp.einsum('bhts,bsc->bthc', p, ckv_all)

    o_lat = map_query_blocks(mla_block, (q_lat, q_rope), q_pos)
    o_mla = jnp.einsum('bthc,chv->bthv', o_lat, lp['mla_w_uv']).reshape(B, T, MLA_WIDTH) * jax.nn.silu(z_mla)

    qd = q_d.reshape(B, T, DIFF_HEADS, 2, DIFF_D)
    kd = k_d.reshape(B, T, DIFF_HEADS, DIFF_V)
    vd = v_d.reshape(B, T, DIFF_HEADS, DIFF_V)
    kd_all = jnp.concatenate([dk_past, kd], axis=1).reshape(B, P + T, DIFF_HEADS, 2, DIFF_D)
    vd_all = jnp.concatenate([dv_past, vd], axis=1)
    lam_init = 0.8 - 0.6 * math.exp(-0.3 * layer_idx)
    lam = (jnp.exp(jnp.sum(lp['diff_lq1'].astype(jnp.float32) * lp['diff_lk1'].astype(jnp.float32)))
           - jnp.exp(jnp.sum(lp['diff_lq2'].astype(jnp.float32) * lp['diff_lk2'].astype(jnp.float32)))
           + lam_init)
    slopes = alibi_slopes(DIFF_HEADS)

    def diff_block(qs, qp):
        (qb,) = qs
        s = jnp.einsum('bthcd,bshcd->bchts', qb, kd_all).astype(jnp.float32) * DIFF_D ** -0.5
        dist = jnp.abs(qp[:, None] - k_pos[None, :]).astype(jnp.float32)
        s = s - slopes[:, None, None] * dist[None]
        s = jnp.where(chunk_visible(qp, k_pos)[None, None, None], s, NEG_INF)
        p = jax.nn.softmax(s, axis=-1)
        a = (p[:, 0] - lam * p[:, 1]).astype(vd_all.dtype)
        return jnp.einsum('bhts,bshv->bthv', a, vd_all)

    o_d = map_query_blocks(diff_block, (qd,), q_pos)
    o_d = rms_norm(o_d, lp['diff_subln']) * (1.0 - lam_init)
    o_diff = o_d.reshape(B, T, DIFF_WIDTH) * jax.nn.silu(z_diff)

    o_lru, h_last, conv_new = rg_lru(x_lru, z_lru, q_pos, conv0, h0, lp)

    g = jax.nn.sigmoid(gate_logits.astype(jnp.float32)).astype(x.dtype).reshape(B, T, N_BRANCH, D_MODEL)
    merged = (g[:, :, 0] * jnp.einsum('btw,wd->btd', o_mla, lp['w_o_mla'])
              + g[:, :, 1] * jnp.einsum('btw,wd->btd', o_diff, lp['w_o_diff'])
              + g[:, :, 2] * jnp.einsum('btw,wd->btd', o_lru, lp['w_o_lru']))
    y = jnp.einsum('btd,de->bte', merged, lp['w_out'])
    return x + y, (c_kv, k_r, kd, vd, h_last, conv_new)


def setup_inputs(seed: int = 0) -> dict:
    key = jax.random.key(seed)
    ks = jax.random.split(key, 32)

    def nrm(i, shape, scale):
        return jax.random.normal(ks[i], shape, jnp.float32) * scale

    def gain(i, shape):
        return 1.0 + nrm(i, shape, 0.02)

    a0 = jax.random.uniform(ks[30], (DEPTH, LRU_WIDTH), jnp.float32, 0.9, 0.999)
    return {
        'x_prompt': nrm(0, (BATCH, SEQ, D_MODEL), 1.0),
        'x_sample': nrm(1, (DEC_BATCH, DEC_SEQ, D_MODEL), 1.0),
        'cache_mla_ckv': nrm(2, (DEPTH, DEC_BATCH, PAST_LEN, MLA_KV_RANK), 1.0),
        'cache_mla_krope': nrm(3, (DEPTH, DEC_BATCH, PAST_LEN, MLA_ROPE), 1.0),
        'cache_diff_k': nrm(4, (DEPTH, DEC_BATCH, PAST_LEN, DIFF_HEADS, DIFF_V), 1.0),
        'cache_diff_v': nrm(5, (DEPTH, DEC_BATCH, PAST_LEN, DIFF_HEADS, DIFF_V), 1.0),
        'state_lru_h': nrm(6, (DEPTH, DEC_BATCH, LRU_WIDTH), 0.5),
        'state_lru_conv': nrm(7, (DEPTH, DEC_BATCH, CONV_W - 1, LRU_WIDTH), 1.0),
        'norm_g': gain(8, (DEPTH, D_MODEL)),
        'w_in': nrm(9, (DEPTH, D_MODEL, N_IN), D_MODEL ** -0.5),
        'mla_q_norm': gain(10, (DEPTH, MLA_Q_RANK)),
        'mla_kv_norm': gain(11, (DEPTH, MLA_KV_RANK)),
        'mla_w_uq': nrm(12, (DEPTH, MLA_Q_RANK, MLA_HEADS * (MLA_NOPE + MLA_ROPE)), MLA_Q_RANK ** -0.5),
        'mla_w_uk': nrm(13, (DEPTH, MLA_KV_RANK, MLA_HEADS, MLA_NOPE), MLA_KV_RANK ** -0.5),
        'mla_w_uv': nrm(14, (DEPTH, MLA_KV_RANK, MLA_HEADS, MLA_V), MLA_KV_RANK ** -0.5),
        'diff_lq1': nrm(15, (DEPTH, DIFF_D), 0.1),
        'diff_lk1': nrm(16, (DEPTH, DIFF_D), 0.1),
        'diff_lq2': nrm(17, (DEPTH, DIFF_D), 0.1),
        'diff_lk2': nrm(18, (DEPTH, DIFF_D), 0.1),
        'diff_subln': gain(19, (DEPTH, DIFF_V)),
        'lru_conv_w': nrm(20, (DEPTH, CONV_W, LRU_WIDTH), CONV_W ** -0.5),
        'lru_conv_b': nrm(21, (DEPTH, LRU_WIDTH), 0.01),
        'lru_w_a': nrm(22, (DEPTH, LRU_BLOCKS, LRU_BW, LRU_BW), LRU_BW ** -0.5),
        'lru_b_a': nrm(23, (DEPTH, LRU_WIDTH), 0.01),
        'lru_w_x': nrm(24, (DEPTH, LRU_BLOCKS, LRU_BW, LRU_BW), LRU_BW ** -0.5),
        'lru_b_x': nrm(25, (DEPTH, LRU_WIDTH), 0.01),
        'lru_lambda': jnp.log(a0) - jnp.log1p(-a0),
        'w_o_mla': nrm(26, (DEPTH, MLA_WIDTH, D_MODEL), MLA_WIDTH ** -0.5),
        'w_o_diff': nrm(27, (DEPTH, DIFF_WIDTH, D_MODEL), DIFF_WIDTH ** -0.5),
        'w_o_lru': nrm(28, (DEPTH, LRU_WIDTH, D_MODEL), LRU_WIDTH ** -0.5),
        'w_out': nrm(29, (DEPTH, D_MODEL, D_MODEL), D_MODEL ** -0.5),
        'final_norm': gain(31, (D_MODEL,)),
    }


def _stacked(states, i):
    return jnp.stack([s[i] for s in states], axis=0)


def reference(x_prompt, x_sample, cache_mla_ckv, cache_mla_krope, cache_diff_k, cache_diff_v,
              state_lru_h, state_lru_conv, norm_g, w_in, mla_q_norm, mla_kv_norm, mla_w_uq, mla_w_uk,
              mla_w_uv, diff_lq1, diff_lk1, diff_lq2, diff_lk2, diff_subln, lru_conv_w, lru_conv_b,
              lru_w_a, lru_b_a, lru_w_x, lru_b_x, lru_lambda, w_o_mla, w_o_diff, w_o_lru, w_out,
              final_norm):
    bp = x_prompt.shape[0]
    dt = x_prompt.dtype
    empty_past = (jnp.zeros((bp, 0, MLA_KV_RANK), dt), jnp.zeros((bp, 0, MLA_ROPE), dt),
                  jnp.zeros((bp, 0, DIFF_HEADS, DIFF_V), dt), jnp.zeros((bp, 0, DIFF_HEADS, DIFF_V), dt),
                  jnp.zeros((bp, LRU_WIDTH), dt), jnp.zeros((bp, CONV_W - 1, LRU_WIDTH), dt))
    prompt_states = []
    sample_states = []
    xp = x_prompt
    xs = x_sample
    for l in range(DEPTH):
        lp = {
            'norm': norm_g[l], 'w_in': w_in[l],
            'mla_q_norm': mla_q_norm[l], 'mla_kv_norm': mla_kv_norm[l],
            'mla_w_uq': mla_w_uq[l], 'mla_w_uk': mla_w_uk[l], 'mla_w_uv': mla_w_uv[l],
            'diff_lq1': diff_lq1[l], 'diff_lk1': diff_lk1[l], 'diff_lq2': diff_lq2[l], 'diff_lk2': diff_lk2[l],
            'diff_subln': diff_subln[l],
            'lru_conv_w': lru_conv_w[l], 'lru_conv_b': lru_conv_b[l],
            'lru_w_a': lru_w_a[l], 'lru_b_a': lru_b_a[l], 'lru_w_x': lru_w_x[l], 'lru_b_x': lru_b_x[l],
            'lru_lambda': lru_lambda[l],
            'w_o_mla': w_o_mla[l], 'w_o_diff': w_o_diff[l], 'w_o_lru': w_o_lru[l], 'w_out': w_out[l],
        }
        xp, st_p = _layer(xp, empty_past, lp, l)
        past = (cache_mla_ckv[l], cache_mla_krope[l], cache_diff_k[l], cache_diff_v[l],
                state_lru_h[l], state_lru_conv[l])
        xs, st_s = _layer(xs, past, lp, l)
        prompt_states.append(st_p)
        sample_states.append(st_s)
    y_prompt = rms_norm(xp, final_norm)
    y_sample = rms_norm(xs, final_norm)
    return (y_prompt, y_sample,
            _stacked(prompt_states, 0), _stacked(prompt_states, 1), _stacked(prompt_states, 2),
            _stacked(prompt_states, 3), _stacked(prompt_states, 4), _stacked(prompt_states, 5),
            _stacked(sample_states, 0), _stacked(sample_states, 1), _stacked(sample_states, 2),
            _stacked(sample_states, 3), _stacked(sample_states, 4), _stacked(sample_states, 5))
```

```python
import functools
import math

import jax
import jax.numpy as jnp
from jax import lax
from jax.experimental import pallas as pl
from jax.experimental.pallas import tpu as pltpu

F32 = jnp.float32
MXU_DTYPE = jnp.bfloat16

D_MODEL = 1024
CHUNK = 64
CHUNK_SHIFT = 6
EPS = 1e-6
NEG_INF = -1e30

MLA_HEADS = 8
MLA_Q_RANK = 256
MLA_KV_RANK = 128
MLA_NOPE = 64
MLA_ROPE = 32
MLA_V = 64
MLA_WIDTH = MLA_HEADS * MLA_V
MLA_QK = MLA_KV_RANK + MLA_ROPE
ROPE_BASE = 10000.0

DIFF_HEADS = 8
DIFF_D = 32
DIFF_V = 2 * DIFF_D
DIFF_WIDTH = DIFF_HEADS * DIFF_V
DIFF_PAIRS = DIFF_HEADS // 2

LRU_WIDTH = 512
LRU_BLOCKS = 8
CONV_W = 4
LRU_C = 8.0
N_BRANCH = 3

V7X_VMEM_BYTES = 64 * 1024 * 1024
LANES = 128
SUBLANES = 8
VMEM_LIMIT = V7X_VMEM_BYTES * 3 // 4

_MAIN = (('c_q', MLA_Q_RANK), ('c_kv', MLA_KV_RANK), ('z_mla', MLA_WIDTH), ('q_d', DIFF_WIDTH),
         ('k_d', DIFF_WIDTH), ('v_d', DIFF_WIDTH), ('z_diff', DIFF_WIDTH), ('x_lru', LRU_WIDTH),
         ('z_lru', LRU_WIDTH))
_MAIN_OFF = {}
_o = 0
for _n, _w in _MAIN:
    _MAIN_OFF[_n] = (_o, _o + _w)
    _o += _w
MAIN_WIDTH = _o


def _params(sem):
    return pltpu.CompilerParams(dimension_semantics=sem, vmem_limit_bytes=VMEM_LIMIT)


def _rms(x, g):
    return x * lax.rsqrt(jnp.mean(x * x, axis=-1, keepdims=True) + EPS) * g


def _silu(z):
    return z * jax.nn.sigmoid(z)


def _mm(a, b):
    return jnp.dot(a.astype(MXU_DTYPE), b.astype(MXU_DTYPE), preferred_element_type=F32)


def _full(shape):
    return pl.BlockSpec(shape, lambda *_: (0,) * len(shape))


def _proj_kernel(x_ref, g_ref, wm_ref, wkr_ref, wkrs_ref, qg_ref, kvg_ref, wuqn_ref, wuqr_ref,
                 wuqrs_ref, wuk_ref, cos_ref, sin_ref,
                 qcat_ref, kcat_ref, ckv_ref, kr_ref, szm_ref, qd_ref, kdt_ref, kd_ref, vd_ref,
                 vdb_ref, szd_ref, xl_ref, szl_ref):
    xb = _rms(x_ref[...], g_ref[...]).astype(MXU_DTYPE)

    def proj(name):
        a, b = _MAIN_OFF[name]
        return jnp.dot(xb, wm_ref[:, a:b], preferred_element_type=F32)

    cos = cos_ref[...]
    sin = sin_ref[...]

    cq = _rms(proj('c_q'), qg_ref[...]).astype(MXU_DTYPE)
    q_nope = jnp.dot(cq, wuqn_ref[...], preferred_element_type=F32)
    q_rot = (jnp.dot(cq, wuqr_ref[...], preferred_element_type=F32) * cos
             + jnp.dot(cq, wuqrs_ref[...], preferred_element_type=F32) * sin)
    q_lat = jnp.dot(q_nope.astype(MXU_DTYPE), wuk_ref[...], preferred_element_type=F32)
    for h in range(MLA_HEADS):
        qcat_ref[h, :, 0:MLA_KV_RANK] = q_lat[:, h * MLA_KV_RANK:(h + 1) * MLA_KV_RANK].astype(qcat_ref.dtype)
        qcat_ref[h, :, MLA_KV_RANK:MLA_QK] = q_rot[:, h * MLA_ROPE:(h + 1) * MLA_ROPE].astype(qcat_ref.dtype)

    ckv = _rms(proj('c_kv'), kvg_ref[...])
    kr = (jnp.dot(xb, wkr_ref[...], preferred_element_type=F32) * cos[:, :MLA_ROPE]
          + jnp.dot(xb, wkrs_ref[...], preferred_element_type=F32) * sin[:, :MLA_ROPE])
    ckv_ref[...] = ckv
    kr_ref[...] = kr
    kcat_ref[:, 0:MLA_KV_RANK] = ckv.astype(kcat_ref.dtype)
    kcat_ref[:, MLA_KV_RANK:MLA_QK] = kr.astype(kcat_ref.dtype)
    szm_ref[...] = _silu(proj('z_mla'))

    qd_ref[...] = proj('q_d').astype(qd_ref.dtype)
    kd = proj('k_d')
    kd_ref[...] = kd
    kdt_ref[...] = kd.T.astype(kdt_ref.dtype)
    vd = proj('v_d')
    vd_ref[...] = vd
    vdb_ref[...] = vd.astype(vdb_ref.dtype)
    szd_ref[...] = _silu(proj('z_diff'))

    xl_ref[...] = proj('x_lru')
    szl_ref[...] = _silu(proj('z_lru'))


def _proj_call(x2, lw, cos_tab, sin_tab, seq_len):
    n = x2.shape[0]
    tm = min(n, 256)
    assert n % tm == 0
    if seq_len % tm == 0:
        per = seq_len // tm
        tab_map = lambda i: (i % per, 0)
    else:
        assert tm % seq_len == 0
        reps = tm // seq_len
        cos_tab = jnp.tile(cos_tab, (reps, 1))
        sin_tab = jnp.tile(sin_tab, (reps, 1))
        tab_map = lambda i: (0, 0)
    row = lambda w: pl.BlockSpec((tm, w), lambda i: (i, 0))
    rope_w = MLA_HEADS * MLA_ROPE
    in_specs = [row(D_MODEL), _full((1, D_MODEL)), _full((D_MODEL, MAIN_WIDTH)),
                _full((D_MODEL, MLA_ROPE)), _full((D_MODEL, MLA_ROPE)),
                _full((1, MLA_Q_RANK)), _full((1, MLA_KV_RANK)),
                _full((MLA_Q_RANK, MLA_HEADS * MLA_NOPE)), _full((MLA_Q_RANK, rope_w)),
                _full((MLA_Q_RANK, rope_w)), _full((MLA_HEADS * MLA_NOPE, MLA_HEADS * MLA_KV_RANK)),
                pl.BlockSpec((tm, rope_w), tab_map), pl.BlockSpec((tm, rope_w), tab_map)]
    out_shape = [jax.ShapeDtypeStruct((MLA_HEADS, n, MLA_QK), MXU_DTYPE),
                 jax.ShapeDtypeStruct((n, MLA_QK), MXU_DTYPE),
                 jax.ShapeDtypeStruct((n, MLA_KV_RANK), F32),
                 jax.ShapeDtypeStruct((n, MLA_ROPE), F32),
                 jax.ShapeDtypeStruct((n, MLA_WIDTH), F32),
                 jax.ShapeDtypeStruct((n, DIFF_WIDTH), MXU_DTYPE),
                 jax.ShapeDtypeStruct((DIFF_WIDTH, n), MXU_DTYPE),
                 jax.ShapeDtypeStruct((n, DIFF_WIDTH), F32),
                 jax.ShapeDtypeStruct((n, DIFF_WIDTH), F32),
                 jax.ShapeDtypeStruct((n, DIFF_WIDTH), MXU_DTYPE),
                 jax.ShapeDtypeStruct((n, DIFF_WIDTH), F32),
                 jax.ShapeDtypeStruct((n, LRU_WIDTH), F32),
                 jax.ShapeDtypeStruct((n, LRU_WIDTH), F32)]
    out_specs = [pl.BlockSpec((MLA_HEADS, tm, MLA_QK), lambda i: (0, i, 0)),
                 row(MLA_QK), row(MLA_KV_RANK), row(MLA_ROPE), row(MLA_WIDTH), row(DIFF_WIDTH),
                 pl.BlockSpec((DIFF_WIDTH, tm), lambda i: (0, i)),
                 row(DIFF_WIDTH), row(DIFF_WIDTH), row(DIFF_WIDTH), row(DIFF_WIDTH),
                 row(LRU_WIDTH), row(LRU_WIDTH)]
    return pl.pallas_call(
        _proj_kernel, grid=(n // tm,), in_specs=in_specs, out_specs=out_specs, out_shape=out_shape,
        compiler_params=_params(("parallel",)), name="proj",
    )(x2, lw['norm'], lw['w_main'], lw['w_kr'], lw['w_kr_sw'], lw['q_norm'], lw['kv_norm'],
      lw['w_uq_nope'], lw['w_uq_rope'], lw['w_uq_rope_sw'], lw['w_uk_bd'], cos_tab, sin_tab)


def _tile_counts(q0, tq, tk, total_keys):
    seen_by_all = jnp.minimum(((q0 >> CHUNK_SHIFT) + 1) * CHUNK, total_keys)
    seen_by_any = jnp.minimum((((q0 + tq - 1) >> CHUNK_SHIFT) + 1) * CHUNK, total_keys)
    return seen_by_all // tk, (seen_by_any + tk - 1) // tk


def _visible(q0, k0, tq, tk, total_keys):
    qpos = q0 + lax.broadcasted_iota(jnp.int32, (tq, tk), 0)
    kpos = k0 + lax.broadcasted_iota(jnp.int32, (tq, tk), 1)
    never = jnp.int32(jnp.iinfo(jnp.int32).max)
    return jnp.where(kpos < total_keys, kpos >> CHUNK_SHIFT, never) <= (qpos >> CHUNK_SHIFT)


def _mla_kernel(q_ref, k_ref, sz_ref, wuv_ref, o_ref, m_ref, l_ref, acc_ref, *, tq, tk, past, total_keys):
    i = pl.program_id(1)
    rows = MLA_HEADS * tq
    q = q_ref[...].reshape(rows, MLA_QK)
    scale = (MLA_NOPE + MLA_ROPE) ** -0.5
    m_ref[...] = jnp.full(m_ref.shape, NEG_INF, F32)
    l_ref[...] = jnp.zeros(l_ref.shape, F32)
    acc_ref[...] = jnp.zeros(acc_ref.shape, F32)
    q0 = past + i * tq
    n_full, n_any = _tile_counts(q0, tq, tk, total_keys)

    def step(j, masked):
        k0 = pl.multiple_of(j * tk, tk)
        k = k_ref[pl.ds(k0, tk), :]
        s = lax.dot_general(q, k, (((1,), (1,)), ((), ())), preferred_element_type=F32) * scale
        if masked:
            vis = _visible(q0, k0, tq, tk, total_keys)
            s = jnp.where(vis[None], s.reshape(MLA_HEADS, tq, tk), NEG_INF).reshape(rows, tk)
        m_prev = m_ref[...]
        m_new = jnp.maximum(m_prev, jnp.max(s, axis=-1, keepdims=True))
        p = jnp.exp(s - m_new)
        alpha = jnp.exp(m_prev - m_new)
        l_ref[...] = alpha * l_ref[...] + jnp.sum(p, axis=-1, keepdims=True)
        acc_ref[...] = alpha * acc_ref[...] + jnp.dot(p.astype(MXU_DTYPE), k[:, :MLA_KV_RANK],
                                                      preferred_element_type=F32)
        m_ref[...] = m_new

    def full_body(j, c):
        step(j, False)
        return c

    def edge_body(j, c):
        step(j, True)
        return c

    lax.fori_loop(0, n_full, full_body, 0)
    lax.fori_loop(n_full, n_any, edge_body, 0)

    o_lat = acc_ref[...] * (1.0 / l_ref[...])
    o_cat = jnp.concatenate([o_lat[h * tq:(h + 1) * tq] for h in range(MLA_HEADS)], axis=1)
    o_ref[...] = jnp.dot(o_cat.astype(MXU_DTYPE), wuv_ref[...], preferred_element_type=F32) * sz_ref[...]


def _mla_call(qcat, kcat_all, sz, w_uv_bd, batch, seq_len, past, keys_padded, tq, tk):
    nq = seq_len // tq
    rows = MLA_HEADS * tq
    kern = functools.partial(_mla_kernel, tq=tq, tk=tk, past=past, total_keys=past + seq_len)
    return pl.pallas_call(
        kern, grid=(batch, nq),
        in_specs=[pl.BlockSpec((MLA_HEADS, tq, MLA_QK), lambda b, i: (0, b * nq + i, 0)),
                  pl.BlockSpec((keys_padded, MLA_QK), lambda b, i: (b, 0)),
                  pl.BlockSpec((tq, MLA_WIDTH), lambda b, i: (b * nq + i, 0)),
                  _full((MLA_HEADS * MLA_KV_RANK, MLA_WIDTH))],
        out_specs=pl.BlockSpec((tq, MLA_WIDTH), lambda b, i: (b * nq + i, 0)),
        out_shape=jax.ShapeDtypeStruct((batch * seq_len, MLA_WIDTH), F32),
        scratch_shapes=[pltpu.VMEM((rows, 1), F32), pltpu.VMEM((rows, 1), F32),
                        pltpu.VMEM((rows, MLA_KV_RANK), F32)],
        compiler_params=_params(("parallel", "parallel")), name="mla",
    )(qcat, kcat_all, sz, w_uv_bd)


def _diff_kernel(lam_init_ref, lamp_ref, slope_ref, q_ref, kt_ref, v_ref, sz_ref, g_ref, o_ref,
                 m_ref, l_ref, acc_ref, *, tq, tk, past, total_keys):
    i = pl.program_id(2)
    n_maps = 4
    scale = DIFF_D ** -0.5
    qf = q_ref[...].astype(F32)
    qs = [qf[:, c * DIFF_D:(c + 1) * DIFF_D].astype(MXU_DTYPE) for c in range(n_maps)]
    slopes = [slope_ref[0:1, 0:1], slope_ref[1:2, 0:1]]
    m_ref[...] = jnp.full(m_ref.shape, NEG_INF, F32)
    l_ref[...] = jnp.zeros(l_ref.shape, F32)
    acc_ref[...] = jnp.zeros(acc_ref.shape, F32)
    q0 = past + i * tq
    n_full, n_any = _tile_counts(q0, tq, tk, total_keys)
    rel = (lax.broadcasted_iota(jnp.int32, (tq, tk), 0)
           - lax.broadcasted_iota(jnp.int32, (tq, tk), 1)).astype(F32)

    def step(j, masked):
        k0 = pl.multiple_of(j * tk, tk)
        kt = kt_ref[:, pl.ds(k0, tk)]
        v = v_ref[pl.ds(k0, tk), :]
        dist = jnp.abs(rel + (q0 - k0).astype(F32))
        bias = [slopes[0] * dist, slopes[1] * dist]
        if masked:
            vis = _visible(q0, k0, tq, tk, total_keys)
        ps, alphas = [], []
        for c in range(n_maps):
            s = jnp.dot(qs[c], kt[c * DIFF_D:(c + 1) * DIFF_D, :], preferred_element_type=F32) * scale
            s = s - bias[c // 2]
            if masked:
                s = jnp.where(vis, s, NEG_INF)
            m_prev = m_ref[c]
            m_new = jnp.maximum(m_prev, jnp.max(s, axis=-1, keepdims=True))
            p = jnp.exp(s - m_new)
            alpha = jnp.exp(m_prev - m_new)
            l_ref[c] = alpha * l_ref[c] + jnp.sum(p, axis=-1, keepdims=True)
            m_ref[c] = m_new
            ps.append(p.astype(MXU_DTYPE))
            alphas.append(alpha)
        pv = jnp.dot(jnp.concatenate(ps, axis=0), v, preferred_element_type=F32)
        acc_ref[...] = jnp.concatenate(alphas, axis=0) * acc_ref[...] + pv

    def full_body(j, c):
        step(j, False)
        return c

    def edge_body(j, c):
        step(j, True)
        return c

    lax.fori_loop(0, n_full, full_body, 0)
    lax.fori_loop(n_full, n_any, edge_body, 0)

    lam_init = lam_init_ref[0]
    lp = lamp_ref[...]
    lam = (jnp.exp(jnp.sum(lp[0:1] * lp[1:2], axis=-1, keepdims=True))
           - jnp.exp(jnp.sum(lp[2:3] * lp[3:4], axis=-1, keepdims=True)) + lam_init)
    outs = []
    for h in range(2):
        a0 = acc_ref[(2 * h) * tq:(2 * h + 1) * tq, :] * (1.0 / l_ref[2 * h])
        a1 = acc_ref[(2 * h + 1) * tq:(2 * h + 2) * tq, :] * (1.0 / l_ref[2 * h + 1])
        outs.append(a0 - lam * a1)
    lane = lax.broadcasted_iota(jnp.int32, (tq, 2 * DIFF_V), 1)
    first = lane < DIFF_V
    o = jnp.where(first, outs[0], outs[1])
    sq = o * o
    ms0 = jnp.sum(jnp.where(first, sq, 0.0), axis=-1, keepdims=True)
    ms1 = jnp.sum(jnp.where(first, 0.0, sq), axis=-1, keepdims=True)
    ms = jnp.where(first, ms0, ms1) * (1.0 / DIFF_V)
    y = o * lax.rsqrt(ms + EPS) * g_ref[...]
    o_ref[...] = y * (1.0 - lam_init) * sz_ref[...]


def _diff_call(lam_init_arr, lam_params, slopes, qd, kdt_all, v_all, sz, subln2, batch, seq_len, past,
               keys_padded, tq, tk):
    nq = seq_len // tq
    pair_w = 2 * DIFF_V
    kern = functools.partial(_diff_kernel, tq=tq, tk=tk, past=past, total_keys=past + seq_len)
    return pl.pallas_call(
        kern, grid=(batch, DIFF_PAIRS, nq),
        in_specs=[pl.BlockSpec(memory_space=pltpu.SMEM),
                  _full((4, DIFF_D)),
                  pl.BlockSpec((None, 2, LANES), lambda b, p, i: (p, 0, 0)),
                  pl.BlockSpec((tq, pair_w), lambda b, p, i: (b * nq + i, p)),
                  pl.BlockSpec((pair_w, keys_padded), lambda b, p, i: (p, b)),
                  pl.BlockSpec((keys_padded, pair_w), lambda b, p, i: (b, p)),
                  pl.BlockSpec((tq, pair_w), lambda b, p, i: (b * nq + i, p)),
                  _full((1, pair_w))],
        out_specs=pl.BlockSpec((tq, pair_w), lambda b, p, i: (b * nq + i, p)),
        out_shape=jax.ShapeDtypeStruct((batch * seq_len, DIFF_WIDTH), F32),
        scratch_shapes=[pltpu.VMEM((4, tq, 1), F32), pltpu.VMEM((4, tq, 1), F32),
                        pltpu.VMEM((4 * tq, pair_w), F32)],
        compiler_params=_params(("parallel", "parallel", "parallel")), name="diff",
    )(lam_init_arr, lam_params, slopes, qd, kdt_all, v_all, sz, subln2)


_EXT_PAD = SUBLANES


def _lru_kernel(x_ref, sz_ref, conv0_ref, h0_ref, cw_ref, cb_ref, wax_ref, ba_ref, bx_ref, lam_ref,
                o_ref, hlast_ref, convnew_ref, ext_ref, a_ref, b_ref, hs_ref, hc_ref, *, tt, past):
    i = pl.program_id(1)
    hist = CONV_W - 1

    @pl.when(i == 0)
    def _():
        hc_ref[...] = h0_ref[...]
        ext_ref[_EXT_PAD - hist:_EXT_PAD, :] = conv0_ref[...]

    ext_ref[_EXT_PAD:_EXT_PAD + tt, :] = x_ref[...]
    cw = cw_ref[...]
    xc = cb_ref[...]
    for k in range(CONV_W):
        xc = xc + ext_ref[_EXT_PAD - hist + k:_EXT_PAD - hist + k + tt, :] * cw[k:k + 1, :]
    gates = jnp.dot(xc.astype(MXU_DTYPE), wax_ref[...], preferred_element_type=F32)
    r = jax.nn.sigmoid(gates[:, :LRU_WIDTH] + ba_ref[...])
    gi = jax.nn.sigmoid(gates[:, LRU_WIDTH:] + bx_ref[...])
    neg_lam = -lam_ref[...]
    softplus = jnp.maximum(neg_lam, 0.0) + jnp.log1p(jnp.exp(-jnp.abs(neg_lam)))
    log_a = -LRU_C * r * softplus
    a = jnp.exp(log_a)
    qpos = past + i * tt + lax.broadcasted_iota(jnp.int32, (tt, LRU_WIDTH), 0)
    mult = jnp.where(qpos == 0, 1.0, jnp.sqrt(1.0 - a * a))
    a_ref[...] = a
    b_ref[...] = mult * gi * xc

    def row(t, h):
        h = a_ref[pl.ds(t, 1), :] * h + b_ref[pl.ds(t, 1), :]
        hs_ref[pl.ds(t, 1), :] = h
        return h

    h = lax.fori_loop(0, tt, row, hc_ref[...], unroll=8)
    hc_ref[...] = h
    o_ref[...] = hs_ref[...] * sz_ref[...]
    hlast_ref[...] = h
    tail = ext_ref[_EXT_PAD + tt - hist:_EXT_PAD + tt, :]
    convnew_ref[...] = tail
    ext_ref[_EXT_PAD - hist:_EXT_PAD, :] = tail


def _lru_call(x_lru, sz, conv0, h0, lw, batch, seq_len, past):
    tt = min(seq_len, 512)
    assert seq_len % tt == 0 and tt >= CONV_W - 1
    nt = seq_len // tt
    hist = CONV_W - 1
    kern = functools.partial(_lru_kernel, tt=tt, past=past)
    tile = pl.BlockSpec((tt, LRU_WIDTH), lambda b, i: (b * nt + i, 0))
    return pl.pallas_call(
        kern, grid=(batch, nt),
        in_specs=[tile, tile,
                  pl.BlockSpec((None, hist, LRU_WIDTH), lambda b, i: (b, 0, 0)),
                  pl.BlockSpec((None, 1, LRU_WIDTH), lambda b, i: (b, 0, 0)),
                  _full((CONV_W, LRU_WIDTH)), _full((1, LRU_WIDTH)),
                  _full((LRU_WIDTH, 2 * LRU_WIDTH)), _full((1, LRU_WIDTH)), _full((1, LRU_WIDTH)),
                  _full((1, LRU_WIDTH))],
        out_specs=[tile,
                   pl.BlockSpec((None, 1, LRU_WIDTH), lambda b, i: (b, 0, 0)),
                   pl.BlockSpec((None, hist, LRU_WIDTH), lambda b, i: (b, 0, 0))],
        out_shape=[jax.ShapeDtypeStruct((batch * seq_len, LRU_WIDTH), F32),
                   jax.ShapeDtypeStruct((batch, 1, LRU_WIDTH), F32),
                   jax.ShapeDtypeStruct((batch, hist, LRU_WIDTH), F32)],
        scratch_shapes=[pltpu.VMEM((_EXT_PAD + tt, LRU_WIDTH), F32), pltpu.VMEM((tt, LRU_WIDTH), F32),
                        pltpu.VMEM((tt, LRU_WIDTH), F32), pltpu.VMEM((tt, LRU_WIDTH), F32),
                        pltpu.VMEM((1, LRU_WIDTH), F32)],
        compiler_params=_params(("parallel", "arbitrary")), name="lru",
    )(x_lru, sz, conv0, h0, lw['conv_w'], lw['conv_b'], lw['w_ax_bd'], lw['b_a'], lw['b_x'], lw['lam'])


def _merge_kernel(x_ref, g_ref, wg_ref, om_ref, od_ref, ol_ref, wom_ref, wod_ref, wol_ref, wout_ref,
                  fg_ref, o_ref, *, final):
    x = x_ref[...]
    xb = _rms(x, g_ref[...]).astype(MXU_DTYPE)
    merged = None
    for b, (br_ref, w_ref) in enumerate(((om_ref, wom_ref), (od_ref, wod_ref), (ol_ref, wol_ref))):
        gate = jax.nn.sigmoid(jnp.dot(xb, wg_ref[:, b * D_MODEL:(b + 1) * D_MODEL],
                                      preferred_element_type=F32))
        term = gate * jnp.dot(br_ref[...].astype(MXU_DTYPE), w_ref[...], preferred_element_type=F32)
        merged = term if merged is None else merged + term
    y = x + jnp.dot(merged.astype(MXU_DTYPE), wout_ref[...], preferred_element_type=F32)
    if final:
        y = _rms(y, fg_ref[...])
    o_ref[...] = y


def _merge_call(x2, o_mla, o_diff, o_lru, lw, final_g, final):
    n = x2.shape[0]
    tm = min(n, 256)
    row = lambda w: pl.BlockSpec((tm, w), lambda i: (i, 0))
    return pl.pallas_call(
        functools.partial(_merge_kernel, final=final), grid=(n // tm,),
        in_specs=[row(D_MODEL), _full((1, D_MODEL)), _full((D_MODEL, N_BRANCH * D_MODEL)),
                  row(MLA_WIDTH), row(DIFF_WIDTH), row(LRU_WIDTH),
                  _full((MLA_WIDTH, D_MODEL)), _full((DIFF_WIDTH, D_MODEL)), _full((LRU_WIDTH, D_MODEL)),
                  _full((D_MODEL, D_MODEL)), _full((1, D_MODEL))],
        out_specs=row(D_MODEL),
        out_shape=jax.ShapeDtypeStruct((n, D_MODEL), F32),
        compiler_params=_params(("parallel",)), name="merge",
    )(x2, lw['norm'], lw['w_gate'], o_mla, o_diff, o_lru, lw['w_o_mla'], lw['w_o_diff'], lw['w_o_lru'],
      lw['w_out'], final_g)


def _block_diag(blocks):
    n, r, c = blocks.shape
    eye = jnp.eye(n, dtype=blocks.dtype)
    return (blocks[:, :, None, :] * eye[:, None, :, None]).reshape(n * r, n * c)


def _swap_halves(w, groups):
    rows, cols = w.shape
    return w.reshape(rows, groups, 2, cols // groups // 2)[:, :, ::-1, :].reshape(rows, cols)


def _layer_weights(l, norm_g, w_in, mla_q_norm, mla_kv_norm, mla_w_uq, mla_w_uk, mla_w_uv, diff_subln,
                   lru_conv_w, lru_conv_b, lru_w_a, lru_b_a, lru_w_x, lru_b_x, lru_lambda, w_o_mla,
                   w_o_diff, w_o_lru, w_out):
    cast = lambda w: w.astype(MXU_DTYPE)
    names = ('c_q', 'c_kv', 'k_r', 'z_mla', 'q_d', 'k_d', 'v_d', 'z_diff', 'x_lru', 'z_lru', 'gate')
    widths = (MLA_Q_RANK, MLA_KV_RANK, MLA_ROPE, MLA_WIDTH, DIFF_WIDTH, DIFF_WIDTH, DIFF_WIDTH, DIFF_WIDTH,
              LRU_WIDTH, LRU_WIDTH, N_BRANCH * D_MODEL)
    cols, off = {}, 0
    for name, w in zip(names, widths):
        cols[name] = w_in[l][:, off:off + w]
        off += w
    uq = mla_w_uq[l].reshape(MLA_Q_RANK, MLA_HEADS, MLA_NOPE + MLA_ROPE)
    uq_rope = uq[:, :, MLA_NOPE:].reshape(MLA_Q_RANK, MLA_HEADS * MLA_ROPE)
    w_kr = cols['k_r']
    return {
        'norm': norm_g[l][None],
        'w_main': cast(jnp.concatenate([cols[n] for n, _ in _MAIN], axis=1)),
        'w_kr': cast(w_kr), 'w_kr_sw': cast(_swap_halves(w_kr, 1)),
        'w_gate': cast(cols['gate']),
        'q_norm': mla_q_norm[l][None], 'kv_norm': mla_kv_norm[l][None],
        'w_uq_nope': cast(uq[:, :, :MLA_NOPE].reshape(MLA_Q_RANK, MLA_HEADS * MLA_NOPE)),
        'w_uq_rope': cast(uq_rope), 'w_uq_rope_sw': cast(_swap_halves(uq_rope, MLA_HEADS)),
        'w_uk_bd': cast(_block_diag(mla_w_uk[l].transpose(1, 2, 0))),
        'w_uv_bd': cast(_block_diag(mla_w_uv[l].transpose(1, 0, 2))),
        'subln2': jnp.tile(diff_subln[l], 2)[None],
        'conv_w': lru_conv_w[l], 'conv_b': lru_conv_b[l][None],
        'w_ax_bd': cast(jnp.concatenate([_block_diag(lru_w_a[l]), _block_diag(lru_w_x[l])], axis=1)),
        'b_a': lru_b_a[l][None], 'b_x': lru_b_x[l][None], 'lam': lru_lambda[l][None],
        'w_o_mla': cast(w_o_mla[l]), 'w_o_diff': cast(w_o_diff[l]), 'w_o_lru': cast(w_o_lru[l]),
        'w_out': cast(w_out[l]),
    }


def _rope_tables(past, seq_len):
    half = MLA_ROPE // 2
    inv = ROPE_BASE ** (-jnp.arange(half, dtype=F32) / half)
    ang = (past + jnp.arange(seq_len, dtype=jnp.int32)).astype(F32)[:, None] * inv[None, :]
    cos, sin = jnp.cos(ang), jnp.sin(ang)
    return (jnp.tile(jnp.concatenate([cos, cos], axis=1), (1, MLA_HEADS)),
            jnp.tile(jnp.concatenate([-sin, sin], axis=1), (1, MLA_HEADS)))


def _pad_keys(new, cache, batch, seq_len, keys_padded):
    if cache is None and keys_padded == seq_len:
        return new
    w = new.shape[-1]
    parts = [new.reshape(batch, seq_len, w)]
    if cache is not None:
        parts.insert(0, cache.astype(new.dtype))
    have = sum(p.shape[1] for p in parts)
    if keys_padded > have:
        parts.append(jnp.zeros((batch, keys_padded - have, w), new.dtype))
    return jnp.concatenate(parts, axis=1).reshape(batch * keys_padded, w)


def _layer(x2, past_state, lw, layer_idx, lam_params, batch, seq_len, past, final_g, final):
    cache_ckv, cache_kr, cache_dk, cache_dv, h0, conv0 = past_state
    tq = min(seq_len, 128)
    tk = 128
    keys_padded = -(-(past + seq_len) // tk) * tk
    cos_tab, sin_tab = _rope_tables(past, seq_len)
    (qcat, kcat, ckv, kr, sz_mla, qd, kdt, kd, vd, vdb, sz_diff, x_lru, sz_lru) = _proj_call(
        x2, lw, cos_tab, sin_tab, seq_len)

    if past:
        cache_kcat = jnp.concatenate([cache_ckv, cache_kr], axis=-1)
        cache_dk2 = cache_dk.reshape(batch, past, DIFF_WIDTH)
        cache_dv2 = cache_dv.reshape(batch, past, DIFF_WIDTH)
        kcat_all = _pad_keys(kcat, cache_kcat, batch, seq_len, keys_padded)
        kdt_all = _pad_keys(kd.astype(MXU_DTYPE), cache_dk2, batch, seq_len, keys_padded).T
        v_all = _pad_keys(vdb, cache_dv2, batch, seq_len, keys_padded)
    else:
        kcat_all = _pad_keys(kcat, None, batch, seq_len, keys_padded)
        kdt_all = kdt if keys_padded == seq_len else _pad_keys(kd.astype(MXU_DTYPE), None, batch, seq_len,
                                                               keys_padded).T
        v_all = _pad_keys(vdb, None, batch, seq_len, keys_padded)

    o_mla = _mla_call(qcat, kcat_all, sz_mla, lw['w_uv_bd'], batch, seq_len, past, keys_padded, tq, tk)

    lam_init = 0.8 - 0.6 * math.exp(-0.3 * layer_idx)
    lam_init_arr = jnp.array([lam_init], F32)
    slope = 2.0 ** (-8.0 * jnp.arange(1, DIFF_HEADS + 1, dtype=F32) / DIFF_HEADS)
    slopes = jnp.broadcast_to(slope.reshape(DIFF_PAIRS, 2, 1), (DIFF_PAIRS, 2, LANES))
    o_diff = _diff_call(lam_init_arr, lam_params, slopes, qd, kdt_all, v_all, sz_diff, lw['subln2'], batch,
                        seq_len, past, keys_padded, tq, tk)

    o_lru, h_last, conv_new = _lru_call(x_lru, sz_lru, conv0, h0[:, None, :], lw, batch, seq_len, past)

    x_new = _merge_call(x2, o_mla, o_diff, o_lru, lw, final_g, final)
    states = (ckv.reshape(batch, seq_len, MLA_KV_RANK), kr.reshape(batch, seq_len, MLA_ROPE),
              kd.reshape(batch, seq_len, DIFF_HEADS, DIFF_V), vd.reshape(batch, seq_len, DIFF_HEADS, DIFF_V),
              h_last[:, 0, :], conv_new)
    return x_new, states


def kernel(x_prompt, x_sample, cache_mla_ckv, cache_mla_krope, cache_diff_k, cache_diff_v, state_lru_h, state_lru_conv, norm_g, w_in, mla_q_norm, mla_kv_norm, mla_w_uq, mla_w_uk, mla_w_uv, diff_lq1, diff_lk1, diff_lq2, diff_lk2, diff_subln, lru_conv_w, lru_conv_b, lru_w_a, lru_b_a, lru_w_x, lru_b_x, lru_lambda, w_o_mla, w_o_diff, w_o_lru, w_out, final_norm):
    depth = w_in.shape[0]
    bp, tp, _ = x_prompt.shape
    bs, ts, _ = x_sample.shape
    past = cache_mla_ckv.shape[2]
    zeros_h = jnp.zeros((bp, LRU_WIDTH), F32)
    zeros_conv = jnp.zeros((bp, CONV_W - 1, LRU_WIDTH), F32)
    xp = x_prompt.reshape(bp * tp, D_MODEL)
    xs = x_sample.reshape(bs * ts, D_MODEL)
    final_g = final_norm[None]
    p_states, s_states = [], []
    for l in range(depth):
        lw = _layer_weights(l, norm_g, w_in, mla_q_norm, mla_kv_norm, mla_w_uq, mla_w_uk, mla_w_uv,
                            diff_subln, lru_conv_w, lru_conv_b, lru_w_a, lru_b_a, lru_w_x, lru_b_x,
                            lru_lambda, w_o_mla, w_o_diff, w_o_lru, w_out)
        lam_params = jnp.stack([diff_lq1[l], diff_lk1[l], diff_lq2[l], diff_lk2[l]])
        final = l == depth - 1
        xp, st_p = _layer(xp, (None, None, None, None, zeros_h, zeros_conv), lw, l, lam_params, bp, tp, 0,
                          final_g, final)
        xs, st_s = _layer(xs, (cache_mla_ckv[l], cache_mla_krope[l], cache_diff_k[l], cache_diff_v[l],
                               state_lru_h[l], state_lru_conv[l]), lw, l, lam_params, bs, ts, past,
                          final_g, final)
        p_states.append(st_p)
        s_states.append(st_s)
    stack = lambda states, i: jnp.stack([s[i] for s in states], axis=0)
    return (xp.reshape(bp, tp, D_MODEL), xs.reshape(bs, ts, D_MODEL),
            *(stack(p_states, i) for i in range(6)), *(stack(s_states, i) for i in range(6)))
```

```python
import functools
import math

import jax
import jax.numpy as jnp
from jax import lax
from jax.experimental import pallas as pl
from jax.experimental.pallas import tpu as pltpu

F32 = jnp.float32
MXU_DTYPE = jnp.bfloat16

D_MODEL = 1024
CHUNK = 64
CHUNK_SHIFT = 6
EPS = 1e-6
NEG_INF = -1e30
LOG2E = math.log2(math.e)

MLA_HEADS = 8
MLA_Q_RANK = 256
MLA_KV_RANK = 128
MLA_NOPE = 64
MLA_ROPE = 32
MLA_V = 64
MLA_WIDTH = MLA_HEADS * MLA_V
MLA_QK = MLA_KV_RANK + MLA_ROPE
ROPE_BASE = 10000.0

DIFF_HEADS = 8
DIFF_D = 32
DIFF_V = 2 * DIFF_D
DIFF_WIDTH = DIFF_HEADS * DIFF_V
DIFF_PAIRS = DIFF_HEADS // 2

LRU_WIDTH = 512
LRU_BLOCKS = 8
CONV_W = 4
LRU_C = 8.0
N_BRANCH = 3

V7X_VMEM_BYTES = 64 * 1024 * 1024
LANES = 128
SUBLANES = 8
VMEM_LIMIT = V7X_VMEM_BYTES * 3 // 4

_MAIN = (('c_q', MLA_Q_RANK), ('c_kv', MLA_KV_RANK), ('z_mla', MLA_WIDTH), ('q_d', DIFF_WIDTH),
         ('k_d', DIFF_WIDTH), ('v_d', DIFF_WIDTH), ('z_diff', DIFF_WIDTH), ('x_lru', LRU_WIDTH),
         ('z_lru', LRU_WIDTH))
_MAIN_OFF = {}
_o = 0
for _n, _w in _MAIN:
    _MAIN_OFF[_n] = (_o, _o + _w)
    _o += _w
MAIN_WIDTH = _o


def _params(sem):
    return pltpu.CompilerParams(dimension_semantics=sem, vmem_limit_bytes=VMEM_LIMIT)


def _rms(x, g):
    return x * lax.rsqrt(jnp.mean(x * x, axis=-1, keepdims=True) + EPS) * g


def _silu(z):
    return z * jax.nn.sigmoid(z)


def _mm(a, b):
    return jnp.dot(a.astype(MXU_DTYPE), b.astype(MXU_DTYPE), preferred_element_type=F32)


def _full(shape):
    return pl.BlockSpec(shape, lambda *_: (0,) * len(shape))


def _proj_kernel(x_ref, g_ref, wm_ref, wkr_ref, wkrs_ref, qg_ref, kvg_ref, wuqn_ref, wuqr_ref,
                 wuqrs_ref, wuk_ref, cos_ref, sin_ref,
                 q_ref, kcat_ref, ckvt_ref, ckv_ref, kr_ref, szm_ref, qd_ref, kdb_ref, kd_ref, vd_ref,
                 vdb_ref, szd_ref, xl_ref, szl_ref, *, keys_on_rows):
    xb = _rms(x_ref[...], g_ref[...]).astype(MXU_DTYPE)

    def proj(name):
        a, b = _MAIN_OFF[name]
        return jnp.dot(xb, wm_ref[:, a:b], preferred_element_type=F32)

    cos = cos_ref[...]
    sin = sin_ref[...]

    cq = _rms(proj('c_q'), qg_ref[...]).astype(MXU_DTYPE)
    q_nope = jnp.dot(cq, wuqn_ref[...], preferred_element_type=F32)
    q_rot = (jnp.dot(cq, wuqr_ref[...], preferred_element_type=F32) * cos
             + jnp.dot(cq, wuqrs_ref[...], preferred_element_type=F32) * sin)
    q_lat = jnp.dot(q_nope.astype(MXU_DTYPE), wuk_ref[...], preferred_element_type=F32)
    if keys_on_rows:
        q_lat_t = q_lat.T.astype(q_ref.dtype)
        q_rot_t = q_rot.T.astype(q_ref.dtype)
        for h in range(MLA_HEADS):
            q_ref[h, 0:MLA_KV_RANK, :] = q_lat_t[h * MLA_KV_RANK:(h + 1) * MLA_KV_RANK]
            q_ref[h, MLA_KV_RANK:MLA_QK, :] = q_rot_t[h * MLA_ROPE:(h + 1) * MLA_ROPE]
    else:
        for h in range(MLA_HEADS):
            q_ref[h, :, 0:MLA_KV_RANK] = q_lat[:, h * MLA_KV_RANK:(h + 1) * MLA_KV_RANK].astype(q_ref.dtype)
            q_ref[h, :, MLA_KV_RANK:MLA_QK] = q_rot[:, h * MLA_ROPE:(h + 1) * MLA_ROPE].astype(q_ref.dtype)

    ckv = _rms(proj('c_kv'), kvg_ref[...])
    kr = (jnp.dot(xb, wkr_ref[...], preferred_element_type=F32) * cos[:, :MLA_ROPE]
          + jnp.dot(xb, wkrs_ref[...], preferred_element_type=F32) * sin[:, :MLA_ROPE])
    ckv_ref[...] = ckv
    kr_ref[...] = kr
    kcat_ref[:, 0:MLA_KV_RANK] = ckv.astype(kcat_ref.dtype)
    kcat_ref[:, MLA_KV_RANK:MLA_QK] = kr.astype(kcat_ref.dtype)
    ckvt_ref[...] = ckv.T.astype(ckvt_ref.dtype)
    szm_ref[...] = _silu(proj('z_mla'))

    qd = proj('q_d')
    kd = proj('k_d')
    vd = proj('v_d')
    kd_ref[...] = kd
    vd_ref[...] = vd
    if keys_on_rows:
        qd_ref[...] = qd.T.astype(qd_ref.dtype)
        kdb_ref[...] = kd.astype(kdb_ref.dtype)
        vdb_ref[...] = vd.T.astype(vdb_ref.dtype)
    else:
        qd_ref[...] = qd.astype(qd_ref.dtype)
        kdb_ref[...] = kd.T.astype(kdb_ref.dtype)
        vdb_ref[...] = vd.astype(vdb_ref.dtype)
    szd_ref[...] = _silu(proj('z_diff'))

    xl_ref[...] = proj('x_lru')
    szl_ref[...] = _silu(proj('z_lru'))


def _proj_call(x2, lw, cos_tab, sin_tab, seq_len, keys_on_rows):
    n = x2.shape[0]
    tm = min(n, 256)
    assert n % tm == 0
    if seq_len % tm == 0:
        per = seq_len // tm
        tab_map = lambda i: (i % per, 0)
    else:
        assert tm % seq_len == 0
        reps = tm // seq_len
        cos_tab = jnp.tile(cos_tab, (reps, 1))
        sin_tab = jnp.tile(sin_tab, (reps, 1))
        tab_map = lambda i: (0, 0)
    row = lambda w: pl.BlockSpec((tm, w), lambda i: (i, 0))
    col = lambda w: pl.BlockSpec((w, tm), lambda i: (0, i))
    rope_w = MLA_HEADS * MLA_ROPE
    in_specs = [row(D_MODEL), _full((1, D_MODEL)), _full((D_MODEL, MAIN_WIDTH)),
                _full((D_MODEL, MLA_ROPE)), _full((D_MODEL, MLA_ROPE)),
                _full((1, MLA_Q_RANK)), _full((1, MLA_KV_RANK)),
                _full((MLA_Q_RANK, MLA_HEADS * MLA_NOPE)), _full((MLA_Q_RANK, rope_w)),
                _full((MLA_Q_RANK, rope_w)), _full((MLA_HEADS * MLA_NOPE, MLA_HEADS * MLA_KV_RANK)),
                pl.BlockSpec((tm, rope_w), tab_map), pl.BlockSpec((tm, rope_w), tab_map)]
    rows_f32 = lambda w: (jax.ShapeDtypeStruct((n, w), F32), row(w))
    rows_mxu = lambda w: (jax.ShapeDtypeStruct((n, w), MXU_DTYPE), row(w))
    cols_mxu = lambda w: (jax.ShapeDtypeStruct((w, n), MXU_DTYPE), col(w))
    if keys_on_rows:
        q_out = (jax.ShapeDtypeStruct((MLA_HEADS, MLA_QK, n), MXU_DTYPE),
                 pl.BlockSpec((MLA_HEADS, MLA_QK, tm), lambda i: (0, 0, i)))
        qd_out, kdb_out, vdb_out = cols_mxu(DIFF_WIDTH), rows_mxu(DIFF_WIDTH), cols_mxu(DIFF_WIDTH)
    else:
        q_out = (jax.ShapeDtypeStruct((MLA_HEADS, n, MLA_QK), MXU_DTYPE),
                 pl.BlockSpec((MLA_HEADS, tm, MLA_QK), lambda i: (0, i, 0)))
        qd_out, kdb_out, vdb_out = rows_mxu(DIFF_WIDTH), cols_mxu(DIFF_WIDTH), rows_mxu(DIFF_WIDTH)
    outs = [q_out, rows_mxu(MLA_QK), cols_mxu(MLA_KV_RANK), rows_f32(MLA_KV_RANK), rows_f32(MLA_ROPE),
            rows_f32(MLA_WIDTH), qd_out, kdb_out, rows_f32(DIFF_WIDTH), rows_f32(DIFF_WIDTH), vdb_out,
            rows_f32(DIFF_WIDTH), rows_f32(LRU_WIDTH), rows_f32(LRU_WIDTH)]
    return pl.pallas_call(
        functools.partial(_proj_kernel, keys_on_rows=keys_on_rows), grid=(n // tm,), in_specs=in_specs,
        out_specs=[o[1] for o in outs], out_shape=[o[0] for o in outs],
        compiler_params=_params(("parallel",)), name="proj",
    )(x2, lw['norm'], lw['w_main'], lw['w_kr'], lw['w_kr_sw'], lw['q_norm'], lw['kv_norm'],
      lw['w_uq_nope'], lw['w_uq_rope'], lw['w_uq_rope_sw'], lw['w_uk_bd'], cos_tab, sin_tab)


def _tile_counts(q0, tq, tk, total_keys):
    seen_by_all = jnp.minimum(((q0 >> CHUNK_SHIFT) + 1) * CHUNK, total_keys)
    seen_by_any = jnp.minimum((((q0 + tq - 1) >> CHUNK_SHIFT) + 1) * CHUNK, total_keys)
    return seen_by_all // tk, (seen_by_any + tk - 1) // tk


def _visible(q0, k0, tq, tk, total_keys):
    qpos = q0 + lax.broadcasted_iota(jnp.int32, (tq, tk), 0)
    kpos = k0 + lax.broadcasted_iota(jnp.int32, (tq, tk), 1)
    never = jnp.int32(jnp.iinfo(jnp.int32).max)
    return jnp.where(kpos < total_keys, kpos >> CHUNK_SHIFT, never) <= (qpos >> CHUNK_SHIFT)


def _visible_t(q0, k0, tq, tk, total_keys, width):
    qpos = q0 + lax.broadcasted_iota(jnp.int32, (tk, width), 1) % tq
    kpos = k0 + lax.broadcasted_iota(jnp.int32, (tk, width), 0)
    never = jnp.int32(jnp.iinfo(jnp.int32).max)
    return jnp.where(kpos < total_keys, kpos >> CHUNK_SHIFT, never) <= (qpos >> CHUNK_SHIFT)


def _mla_kernel(q_ref, k_ref, sz_ref, wuv_ref, o_ref, m_ref, l_ref, acc_ref, *, tq, tk, past, total_keys):
    i = pl.program_id(1)
    rows = MLA_HEADS * tq
    q = q_ref[...].reshape(rows, MLA_QK)
    scale = (MLA_NOPE + MLA_ROPE) ** -0.5
    m_ref[...] = jnp.full(m_ref.shape, NEG_INF, F32)
    l_ref[...] = jnp.zeros(l_ref.shape, F32)
    acc_ref[...] = jnp.zeros(acc_ref.shape, F32)
    q0 = past + i * tq
    n_full, n_any = _tile_counts(q0, tq, tk, total_keys)

    def step(j, masked):
        k0 = pl.multiple_of(j * tk, tk)
        k = k_ref[pl.ds(k0, tk), :]
        s = lax.dot_general(q, k, (((1,), (1,)), ((), ())), preferred_element_type=F32) * scale
        if masked:
            vis = _visible(q0, k0, tq, tk, total_keys)
            s = jnp.where(vis[None], s.reshape(MLA_HEADS, tq, tk), NEG_INF).reshape(rows, tk)
        m_prev = m_ref[...]
        m_new = jnp.maximum(m_prev, jnp.max(s, axis=-1, keepdims=True))
        p = jnp.exp(s - m_new)
        alpha = jnp.exp(m_prev - m_new)
        l_ref[...] = alpha * l_ref[...] + jnp.sum(p, axis=-1, keepdims=True)
        acc_ref[...] = alpha * acc_ref[...] + jnp.dot(p.astype(MXU_DTYPE), k[:, :MLA_KV_RANK],
                                                      preferred_element_type=F32)
        m_ref[...] = m_new

    def full_body(j, c):
        step(j, False)
        return c

    def edge_body(j, c):
        step(j, True)
        return c

    lax.fori_loop(0, n_full, full_body, 0)
    lax.fori_loop(n_full, n_any, edge_body, 0)

    o_lat = acc_ref[...] * (1.0 / l_ref[...])
    o_cat = jnp.concatenate([o_lat[h * tq:(h + 1) * tq] for h in range(MLA_HEADS)], axis=1)
    o_ref[...] = jnp.dot(o_cat.astype(MXU_DTYPE), wuv_ref[...], preferred_element_type=F32) * sz_ref[...]


def _mla_call(qcat, kcat_all, sz, w_uv_bd, batch, seq_len, past, keys_padded, tq, tk):
    nq = seq_len // tq
    rows = MLA_HEADS * tq
    kern = functools.partial(_mla_kernel, tq=tq, tk=tk, past=past, total_keys=past + seq_len)
    return pl.pallas_call(
        kern, grid=(batch, nq),
        in_specs=[pl.BlockSpec((MLA_HEADS, tq, MLA_QK), lambda b, i: (0, b * nq + i, 0)),
                  pl.BlockSpec((keys_padded, MLA_QK), lambda b, i: (b, 0)),
                  pl.BlockSpec((tq, MLA_WIDTH), lambda b, i: (b * nq + i, 0)),
                  _full((MLA_HEADS * MLA_KV_RANK, MLA_WIDTH))],
        out_specs=pl.BlockSpec((tq, MLA_WIDTH), lambda b, i: (b * nq + i, 0)),
        out_shape=jax.ShapeDtypeStruct((batch * seq_len, MLA_WIDTH), F32),
        scratch_shapes=[pltpu.VMEM((rows, 1), F32), pltpu.VMEM((rows, 1), F32),
                        pltpu.VMEM((rows, MLA_KV_RANK), F32)],
        compiler_params=_params(("parallel", "parallel")), name="mla",
    )(qcat, kcat_all, sz, w_uv_bd)


def _diff_kernel(lam_init_ref, lamp_ref, slope_ref, q_ref, kt_ref, v_ref, sz_ref, g_ref, o_ref,
                 m_ref, l_ref, acc_ref, *, tq, tk, past, total_keys):
    i = pl.program_id(2)
    n_maps = 4
    scale = DIFF_D ** -0.5
    qf = q_ref[...].astype(F32)
    qs = [qf[:, c * DIFF_D:(c + 1) * DIFF_D].astype(MXU_DTYPE) for c in range(n_maps)]
    slopes = [slope_ref[0:1, 0:1], slope_ref[1:2, 0:1]]
    m_ref[...] = jnp.full(m_ref.shape, NEG_INF, F32)
    l_ref[...] = jnp.zeros(l_ref.shape, F32)
    acc_ref[...] = jnp.zeros(acc_ref.shape, F32)
    q0 = past + i * tq
    n_full, n_any = _tile_counts(q0, tq, tk, total_keys)
    rel = (lax.broadcasted_iota(jnp.int32, (tq, tk), 0)
           - lax.broadcasted_iota(jnp.int32, (tq, tk), 1)).astype(F32)

    def step(j, masked):
        k0 = pl.multiple_of(j * tk, tk)
        kt = kt_ref[:, pl.ds(k0, tk)]
        v = v_ref[pl.ds(k0, tk), :]
        dist = jnp.abs(rel + (q0 - k0).astype(F32))
        bias = [slopes[0] * dist, slopes[1] * dist]
        if masked:
            vis = _visible(q0, k0, tq, tk, total_keys)
        ps, alphas = [], []
        for c in range(n_maps):
            s = jnp.dot(qs[c], kt[c * DIFF_D:(c + 1) * DIFF_D, :], preferred_element_type=F32) * scale
            s = s - bias[c // 2]
            if masked:
                s = jnp.where(vis, s, NEG_INF)
            m_prev = m_ref[c]
            m_new = jnp.maximum(m_prev, jnp.max(s, axis=-1, keepdims=True))
            p = jnp.exp(s - m_new)
            alpha = jnp.exp(m_prev - m_new)
            l_ref[c] = alpha * l_ref[c] + jnp.sum(p, axis=-1, keepdims=True)
            m_ref[c] = m_new
            ps.append(p.astype(MXU_DTYPE))
            alphas.append(alpha)
        pv = jnp.dot(jnp.concatenate(ps, axis=0), v, preferred_element_type=F32)
        acc_ref[...] = jnp.concatenate(alphas, axis=0) * acc_ref[...] + pv

    def full_body(j, c):
        step(j, False)
        return c

    def edge_body(j, c):
        step(j, True)
        return c

    lax.fori_loop(0, n_full, full_body, 0)
    lax.fori_loop(n_full, n_any, edge_body, 0)

    lam_init = lam_init_ref[0]
    lp = lamp_ref[...]
    lam = (jnp.exp(jnp.sum(lp[0:1] * lp[1:2], axis=-1, keepdims=True))
           - jnp.exp(jnp.sum(lp[2:3] * lp[3:4], axis=-1, keepdims=True)) + lam_init)
    outs = []
    for h in range(2):
        a0 = acc_ref[(2 * h) * tq:(2 * h + 1) * tq, :] * (1.0 / l_ref[2 * h])
        a1 = acc_ref[(2 * h + 1) * tq:(2 * h + 2) * tq, :] * (1.0 / l_ref[2 * h + 1])
        outs.append(a0 - lam * a1)
    lane = lax.broadcasted_iota(jnp.int32, (tq, 2 * DIFF_V), 1)
    first = lane < DIFF_V
    o = jnp.where(first, outs[0], outs[1])
    sq = o * o
    ms0 = jnp.sum(jnp.where(first, sq, 0.0), axis=-1, keepdims=True)
    ms1 = jnp.sum(jnp.where(first, 0.0, sq), axis=-1, keepdims=True)
    ms = jnp.where(first, ms0, ms1) * (1.0 / DIFF_V)
    y = o * lax.rsqrt(ms + EPS) * g_ref[...]
    o_ref[...] = y * (1.0 - lam_init) * sz_ref[...]


def _diff_call(lam_init_arr, lam_params, slopes, qd, kdt_all, v_all, sz, subln2, batch, seq_len, past,
               keys_padded, tq, tk):
    nq = seq_len // tq
    pair_w = 2 * DIFF_V
    kern = functools.partial(_diff_kernel, tq=tq, tk=tk, past=past, total_keys=past + seq_len)
    return pl.pallas_call(
        kern, grid=(batch, DIFF_PAIRS, nq),
        in_specs=[pl.BlockSpec(memory_space=pltpu.SMEM),
                  _full((4, DIFF_D)),
                  pl.BlockSpec((None, 2, LANES), lambda b, p, i: (p, 0, 0)),
                  pl.BlockSpec((tq, pair_w), lambda b, p, i: (b * nq + i, p)),
                  pl.BlockSpec((pair_w, keys_padded), lambda b, p, i: (p, b)),
                  pl.BlockSpec((keys_padded, pair_w), lambda b, p, i: (b, p)),
                  pl.BlockSpec((tq, pair_w), lambda b, p, i: (b * nq + i, p)),
                  _full((1, pair_w))],
        out_specs=pl.BlockSpec((tq, pair_w), lambda b, p, i: (b * nq + i, p)),
        out_shape=jax.ShapeDtypeStruct((batch * seq_len, DIFF_WIDTH), F32),
        scratch_shapes=[pltpu.VMEM((4, tq, 1), F32), pltpu.VMEM((4, tq, 1), F32),
                        pltpu.VMEM((4 * tq, pair_w), F32)],
        compiler_params=_params(("parallel", "parallel", "parallel")), name="diff",
    )(lam_init_arr, lam_params, slopes, qd, kdt_all, v_all, sz, subln2)


MXU_COLS = 256


def _mla_kr_kernel(q_ref, k_ref, vt_ref, sz_ref, wuvt_ref, o_ref, m_ref, l_ref, acc_ref, *, tq, tk, past,
                   total_keys):
    i = pl.program_id(1)
    cols = MLA_HEADS * tq
    heads_per_group = MXU_COLS // tq
    n_groups = cols // MXU_COLS
    c1 = (MLA_NOPE + MLA_ROPE) ** -0.5 * LOG2E
    m_ref[...] = jnp.full(m_ref.shape, NEG_INF, F32)
    l_ref[...] = jnp.zeros(l_ref.shape, F32)
    acc_ref[...] = jnp.zeros(acc_ref.shape, F32)
    q0 = past + i * tq
    n_full, n_any = _tile_counts(q0, tq, tk, total_keys)

    def step(j, masked):
        k0 = pl.multiple_of(j * tk, tk)
        k = k_ref[pl.ds(k0, tk), :]
        vt = vt_ref[:, pl.ds(k0, tk)]
        if masked:
            vis = _visible_t(q0, k0, tq, tk, total_keys, MXU_COLS)
        lanes = [slice(g * MXU_COLS, (g + 1) * MXU_COLS) for g in range(n_groups)]
        scores = [jnp.dot(k, jnp.concatenate([q_ref[g * heads_per_group + h] for h in range(heads_per_group)],
                                             axis=1), preferred_element_type=F32) for g in range(n_groups)]
        probs, alphas = [], []
        for g in range(n_groups):
            s = jnp.where(vis, scores[g], NEG_INF) if masked else scores[g]
            m_prev = m_ref[:, lanes[g]]
            m_new = jnp.maximum(m_prev, jnp.max(s, axis=0, keepdims=True))
            p = jnp.exp2((s - m_new) * c1)
            alpha = jnp.exp2((m_prev - m_new) * c1)
            l_ref[:, lanes[g]] = alpha * l_ref[:, lanes[g]] + jnp.sum(p, axis=0, keepdims=True)
            m_ref[:, lanes[g]] = m_new
            probs.append(p.astype(MXU_DTYPE))
            alphas.append(alpha)
        for g in range(n_groups):
            acc_ref[:, lanes[g]] = alphas[g] * acc_ref[:, lanes[g]] + jnp.dot(vt, probs[g],
                                                                              preferred_element_type=F32)

    def full_body(j, c):
        step(j, False)
        return c

    def edge_body(j, c):
        step(j, True)
        return c

    lax.fori_loop(0, n_full, full_body, 0)
    lax.fori_loop(n_full, n_any, edge_body, 0)

    o_t = acc_ref[...] * (1.0 / l_ref[...])
    o_cat_t = jnp.concatenate([o_t[:, h * tq:(h + 1) * tq] for h in range(MLA_HEADS)], axis=0)
    out_t = jnp.dot(wuvt_ref[...], o_cat_t.astype(MXU_DTYPE), preferred_element_type=F32)
    o_ref[...] = out_t.T * sz_ref[...]


def _mla_kr_call(q_t, kcat_all, ckvt_all, sz, w_uv_bd_t, batch, seq_len, past, keys_padded, tq, tk):
    nq = seq_len // tq
    cols = MLA_HEADS * tq
    kern = functools.partial(_mla_kr_kernel, tq=tq, tk=tk, past=past, total_keys=past + seq_len)
    return pl.pallas_call(
        kern, grid=(batch, nq),
        in_specs=[pl.BlockSpec((MLA_HEADS, MLA_QK, tq), lambda b, i: (0, 0, b * nq + i)),
                  pl.BlockSpec((keys_padded, MLA_QK), lambda b, i: (b, 0)),
                  pl.BlockSpec((MLA_KV_RANK, keys_padded), lambda b, i: (0, b)),
                  pl.BlockSpec((tq, MLA_WIDTH), lambda b, i: (b * nq + i, 0)),
                  _full((MLA_WIDTH, MLA_HEADS * MLA_KV_RANK))],
        out_specs=pl.BlockSpec((tq, MLA_WIDTH), lambda b, i: (b * nq + i, 0)),
        out_shape=jax.ShapeDtypeStruct((batch * seq_len, MLA_WIDTH), F32),
        scratch_shapes=[pltpu.VMEM((1, cols), F32), pltpu.VMEM((1, cols), F32),
                        pltpu.VMEM((MLA_KV_RANK, cols), F32)],
        compiler_params=_params(("parallel", "parallel")), name="mla_kr",
    )(q_t, kcat_all, ckvt_all, sz, w_uv_bd_t)


def _diff_kr_kernel(lam_init_ref, lamp_ref, slope_ref, qt_ref, k_ref, vt_ref, sz_ref, g_ref, o_ref,
                    m_ref, l_ref, acc_ref, *, tq, tk, past, total_keys):
    i = pl.program_id(2)
    n_maps = 4
    pair_w = 2 * DIFF_V
    c1 = DIFF_D ** -0.5 * LOG2E
    qt = qt_ref[...].astype(F32)
    feat_map = lax.broadcasted_iota(jnp.int32, (pair_w, tq), 0) // DIFF_D
    w_all = jnp.concatenate([jnp.where(feat_map == c, qt, 0.0) for c in range(n_maps)],
                            axis=1).astype(MXU_DTYPE)
    slope2 = [slope_ref[0:1, 0:1] * LOG2E, slope_ref[1:2, 0:1] * LOG2E]
    m_ref[...] = jnp.full(m_ref.shape, NEG_INF, F32)
    l_ref[...] = jnp.zeros(l_ref.shape, F32)
    acc_ref[...] = jnp.zeros(acc_ref.shape, F32)
    q0 = past + i * tq
    n_full, n_any = _tile_counts(q0, tq, tk, total_keys)
    rel = (lax.broadcasted_iota(jnp.int32, (tk, tq), 0)
           - lax.broadcasted_iota(jnp.int32, (tk, tq), 1)).astype(F32)

    def step(j, masked):
        k0 = pl.multiple_of(j * tk, tk)
        k = k_ref[pl.ds(k0, tk), :]
        vt = vt_ref[:, pl.ds(k0, tk)]
        s_all = jnp.dot(k, w_all, preferred_element_type=F32)
        dist = jnp.abs(rel + (k0 - q0).astype(F32))
        bias = [slope2[0] * dist, slope2[1] * dist]
        if masked:
            vis = _visible_t(q0, k0, tq, tk, total_keys, tq)
        ps, alphas = [], []
        for c in range(n_maps):
            lanes = slice(c * tq, (c + 1) * tq)
            u = s_all[:, lanes] * c1 - bias[c // 2]
            if masked:
                u = jnp.where(vis, u, NEG_INF)
            m_prev = m_ref[:, lanes]
            m_new = jnp.maximum(m_prev, jnp.max(u, axis=0, keepdims=True))
            p = jnp.exp2(u - m_new)
            alpha = jnp.exp2(m_prev - m_new)
            l_ref[:, lanes] = alpha * l_ref[:, lanes] + jnp.sum(p, axis=0, keepdims=True)
            m_ref[:, lanes] = m_new
            ps.append(p.astype(MXU_DTYPE))
            alphas.append(alpha)
        pv = jnp.dot(vt, jnp.concatenate(ps, axis=1), preferred_element_type=F32)
        acc_ref[...] = jnp.concatenate(alphas, axis=1) * acc_ref[...] + pv

    def full_body(j, c):
        step(j, False)
        return c

    def edge_body(j, c):
        step(j, True)
        return c

    lax.fori_loop(0, n_full, full_body, 0)
    lax.fori_loop(n_full, n_any, edge_body, 0)

    lam_init = lam_init_ref[0]
    lp = lamp_ref[...]
    lam = (jnp.exp(jnp.sum(lp[0:1] * lp[1:2], axis=-1, keepdims=True))
           - jnp.exp(jnp.sum(lp[2:3] * lp[3:4], axis=-1, keepdims=True)) + lam_init)
    a = acc_ref[...] * (1.0 / l_ref[...])
    halves = []
    for h in range(2):
        rows = slice(h * DIFF_V, (h + 1) * DIFF_V)
        halves.append(a[rows, (2 * h) * tq:(2 * h + 1) * tq] - lam * a[rows, (2 * h + 1) * tq:(2 * h + 2) * tq])
    o = jnp.concatenate(halves, axis=0).T
    lane = lax.broadcasted_iota(jnp.int32, (tq, pair_w), 1)
    first = lane < DIFF_V
    sq = o * o
    ms0 = jnp.sum(jnp.where(first, sq, 0.0), axis=-1, keepdims=True)
    ms1 = jnp.sum(jnp.where(first, 0.0, sq), axis=-1, keepdims=True)
    ms = jnp.where(first, ms0, ms1) * (1.0 / DIFF_V)
    y = o * lax.rsqrt(ms + EPS) * g_ref[...]
    o_ref[...] = y * (1.0 - lam_init) * sz_ref[...]


def _diff_kr_call(lam_init_arr, lam_params, slopes, qd_t, k_all, vt_all, sz, subln2, batch, seq_len, past,
                  keys_padded, tq, tk):
    nq = seq_len // tq
    pair_w = 2 * DIFF_V
    kern = functools.partial(_diff_kr_kernel, tq=tq, tk=tk, past=past, total_keys=past + seq_len)
    return pl.pallas_call(
        kern, grid=(batch, DIFF_PAIRS, nq),
        in_specs=[pl.BlockSpec(memory_space=pltpu.SMEM),
                  _full((4, DIFF_D)),
                  pl.BlockSpec((None, 2, LANES), lambda b, p, i: (p, 0, 0)),
                  pl.BlockSpec((pair_w, tq), lambda b, p, i: (p, b * nq + i)),
                  pl.BlockSpec((keys_padded, pair_w), lambda b, p, i: (b, p)),
                  pl.BlockSpec((pair_w, keys_padded), lambda b, p, i: (p, b)),
                  pl.BlockSpec((tq, pair_w), lambda b, p, i: (b * nq + i, p)),
                  _full((1, pair_w))],
        out_specs=pl.BlockSpec((tq, pair_w), lambda b, p, i: (b * nq + i, p)),
        out_shape=jax.ShapeDtypeStruct((batch * seq_len, DIFF_WIDTH), F32),
        scratch_shapes=[pltpu.VMEM((1, 4 * tq), F32), pltpu.VMEM((1, 4 * tq), F32),
                        pltpu.VMEM((pair_w, 4 * tq), F32)],
        compiler_params=_params(("parallel", "parallel", "parallel")), name="diff_kr",
    )(lam_init_arr, lam_params, slopes, qd_t, k_all, vt_all, sz, subln2)


_EXT_PAD = SUBLANES


def _lru_kernel(x_ref, sz_ref, conv0_ref, h0_ref, cw_ref, cb_ref, wax_ref, ba_ref, bx_ref, lam_ref,
                o_ref, hlast_ref, convnew_ref, ext_ref, a_ref, b_ref, hs_ref, hc_ref, *, tt, past):
    i = pl.program_id(1)
    hist = CONV_W - 1

    @pl.when(i == 0)
    def _():
        hc_ref[...] = h0_ref[...]
        ext_ref[_EXT_PAD - hist:_EXT_PAD, :] = conv0_ref[...]

    ext_ref[_EXT_PAD:_EXT_PAD + tt, :] = x_ref[...]
    cw = cw_ref[...]
    xc = cb_ref[...]
    for k in range(CONV_W):
        xc = xc + ext_ref[_EXT_PAD - hist + k:_EXT_PAD - hist + k + tt, :] * cw[k:k + 1, :]
    gates = jnp.dot(xc.astype(MXU_DTYPE), wax_ref[...], preferred_element_type=F32)
    r = jax.nn.sigmoid(gates[:, :LRU_WIDTH] + ba_ref[...])
    gi = jax.nn.sigmoid(gates[:, LRU_WIDTH:] + bx_ref[...])
    neg_lam = -lam_ref[...]
    softplus = jnp.maximum(neg_lam, 0.0) + jnp.log1p(jnp.exp(-jnp.abs(neg_lam)))
    log_a = -LRU_C * r * softplus
    a = jnp.exp(log_a)
    qpos = past + i * tt + lax.broadcasted_iota(jnp.int32, (tt, LRU_WIDTH), 0)
    mult = jnp.where(qpos == 0, 1.0, jnp.sqrt(1.0 - a * a))
    a_ref[...] = a
    b_ref[...] = mult * gi * xc

    def row(t, h):
        h = a_ref[pl.ds(t, 1), :] * h + b_ref[pl.ds(t, 1), :]
        hs_ref[pl.ds(t, 1), :] = h
        return h

    h = lax.fori_loop(0, tt, row, hc_ref[...], unroll=8)
    hc_ref[...] = h
    o_ref[...] = hs_ref[...] * sz_ref[...]
    hlast_ref[...] = h
    tail = ext_ref[_EXT_PAD + tt - hist:_EXT_PAD + tt, :]
    convnew_ref[...] = tail
    ext_ref[_EXT_PAD - hist:_EXT_PAD, :] = tail


def _lru_call(x_lru, sz, conv0, h0, lw, batch, seq_len, past):
    tt = min(seq_len, 512)
    assert seq_len % tt == 0 and tt >= CONV_W - 1
    nt = seq_len // tt
    hist = CONV_W - 1
    kern = functools.partial(_lru_kernel, tt=tt, past=past)
    tile = pl.BlockSpec((tt, LRU_WIDTH), lambda b, i: (b * nt + i, 0))
    return pl.pallas_call(
        kern, grid=(batch, nt),
        in_specs=[tile, tile,
                  pl.BlockSpec((None, hist, LRU_WIDTH), lambda b, i: (b, 0, 0)),
                  pl.BlockSpec((None, 1, LRU_WIDTH), lambda b, i: (b, 0, 0)),
                  _full((CONV_W, LRU_WIDTH)), _full((1, LRU_WIDTH)),
                  _full((LRU_WIDTH, 2 * LRU_WIDTH)), _full((1, LRU_WIDTH)), _full((1, LRU_WIDTH)),
                  _full((1, LRU_WIDTH))],
        out_specs=[tile,
                   pl.BlockSpec((None, 1, LRU_WIDTH), lambda b, i: (b, 0, 0)),
                   pl.BlockSpec((None, hist, LRU_WIDTH), lambda b, i: (b, 0, 0))],
        out_shape=[jax.ShapeDtypeStruct((batch * seq_len, LRU_WIDTH), F32),
                   jax.ShapeDtypeStruct((batch, 1, LRU_WIDTH), F32),
                   jax.ShapeDtypeStruct((batch, hist, LRU_WIDTH), F32)],
        scratch_shapes=[pltpu.VMEM((_EXT_PAD + tt, LRU_WIDTH), F32), pltpu.VMEM((tt, LRU_WIDTH), F32),
                        pltpu.VMEM((tt, LRU_WIDTH), F32), pltpu.VMEM((tt, LRU_WIDTH), F32),
                        pltpu.VMEM((1, LRU_WIDTH), F32)],
        compiler_params=_params(("parallel", "arbitrary")), name="lru",
    )(x_lru, sz, conv0, h0, lw['conv_w'], lw['conv_b'], lw['w_ax_bd'], lw['b_a'], lw['b_x'], lw['lam'])


def _merge_kernel(x_ref, g_ref, wg_ref, om_ref, od_ref, ol_ref, wom_ref, wod_ref, wol_ref, wout_ref,
                  fg_ref, o_ref, *, final):
    x = x_ref[...]
    xb = _rms(x, g_ref[...]).astype(MXU_DTYPE)
    merged = None
    for b, (br_ref, w_ref) in enumerate(((om_ref, wom_ref), (od_ref, wod_ref), (ol_ref, wol_ref))):
        gate = jax.nn.sigmoid(jnp.dot(xb, wg_ref[:, b * D_MODEL:(b + 1) * D_MODEL],
                                      preferred_element_type=F32))
        term = gate * jnp.dot(br_ref[...].astype(MXU_DTYPE), w_ref[...], preferred_element_type=F32)
        merged = term if merged is None else merged + term
    y = x + jnp.dot(merged.astype(MXU_DTYPE), wout_ref[...], preferred_element_type=F32)
    if final:
        y = _rms(y, fg_ref[...])
    o_ref[...] = y


def _merge_call(x2, o_mla, o_diff, o_lru, lw, final_g, final):
    n = x2.shape[0]
    tm = min(n, 256)
    row = lambda w: pl.BlockSpec((tm, w), lambda i: (i, 0))
    return pl.pallas_call(
        functools.partial(_merge_kernel, final=final), grid=(n // tm,),
        in_specs=[row(D_MODEL), _full((1, D_MODEL)), _full((D_MODEL, N_BRANCH * D_MODEL)),
                  row(MLA_WIDTH), row(DIFF_WIDTH), row(LRU_WIDTH),
                  _full((MLA_WIDTH, D_MODEL)), _full((DIFF_WIDTH, D_MODEL)), _full((LRU_WIDTH, D_MODEL)),
                  _full((D_MODEL, D_MODEL)), _full((1, D_MODEL))],
        out_specs=row(D_MODEL),
        out_shape=jax.ShapeDtypeStruct((n, D_MODEL), F32),
        compiler_params=_params(("parallel",)), name="merge",
    )(x2, lw['norm'], lw['w_gate'], o_mla, o_diff, o_lru, lw['w_o_mla'], lw['w_o_diff'], lw['w_o_lru'],
      lw['w_out'], final_g)


def _block_diag(blocks):
    n, r, c = blocks.shape
    eye = jnp.eye(n, dtype=blocks.dtype)
    return (blocks[:, :, None, :] * eye[:, None, :, None]).reshape(n * r, n * c)


def _swap_halves(w, groups):
    rows, cols = w.shape
    return w.reshape(rows, groups, 2, cols // groups // 2)[:, :, ::-1, :].reshape(rows, cols)


def _layer_weights(l, norm_g, w_in, mla_q_norm, mla_kv_norm, mla_w_uq, mla_w_uk, mla_w_uv, diff_subln,
                   lru_conv_w, lru_conv_b, lru_w_a, lru_b_a, lru_w_x, lru_b_x, lru_lambda, w_o_mla,
                   w_o_diff, w_o_lru, w_out):
    cast = lambda w: w.astype(MXU_DTYPE)
    names = ('c_q', 'c_kv', 'k_r', 'z_mla', 'q_d', 'k_d', 'v_d', 'z_diff', 'x_lru', 'z_lru', 'gate')
    widths = (MLA_Q_RANK, MLA_KV_RANK, MLA_ROPE, MLA_WIDTH, DIFF_WIDTH, DIFF_WIDTH, DIFF_WIDTH, DIFF_WIDTH,
              LRU_WIDTH, LRU_WIDTH, N_BRANCH * D_MODEL)
    cols, off = {}, 0
    for name, w in zip(names, widths):
        cols[name] = w_in[l][:, off:off + w]
        off += w
    uq = mla_w_uq[l].reshape(MLA_Q_RANK, MLA_HEADS, MLA_NOPE + MLA_ROPE)
    uq_rope = uq[:, :, MLA_NOPE:].reshape(MLA_Q_RANK, MLA_HEADS * MLA_ROPE)
    w_kr = cols['k_r']
    return {
        'norm': norm_g[l][None],
        'w_main': cast(jnp.concatenate([cols[n] for n, _ in _MAIN], axis=1)),
        'w_kr': cast(w_kr), 'w_kr_sw': cast(_swap_halves(w_kr, 1)),
        'w_gate': cast(cols['gate']),
        'q_norm': mla_q_norm[l][None], 'kv_norm': mla_kv_norm[l][None],
        'w_uq_nope': cast(uq[:, :, :MLA_NOPE].reshape(MLA_Q_RANK, MLA_HEADS * MLA_NOPE)),
        'w_uq_rope': cast(uq_rope), 'w_uq_rope_sw': cast(_swap_halves(uq_rope, MLA_HEADS)),
        'w_uk_bd': cast(_block_diag(mla_w_uk[l].transpose(1, 2, 0))),
        'w_uv_bd': cast(_block_diag(mla_w_uv[l].transpose(1, 0, 2))),
        'subln2': jnp.tile(diff_subln[l], 2)[None],
        'conv_w': lru_conv_w[l], 'conv_b': lru_conv_b[l][None],
        'w_ax_bd': cast(jnp.concatenate([_block_diag(lru_w_a[l]), _block_diag(lru_w_x[l])], axis=1)),
        'b_a': lru_b_a[l][None], 'b_x': lru_b_x[l][None], 'lam': lru_lambda[l][None],
        'w_o_mla': cast(w_o_mla[l]), 'w_o_diff': cast(w_o_diff[l]), 'w_o_lru': cast(w_o_lru[l]),
        'w_out': cast(w_out[l]),
    }


def _rope_tables(past, seq_len):
    half = MLA_ROPE // 2
    inv = ROPE_BASE ** (-jnp.arange(half, dtype=F32) / half)
    ang = (past + jnp.arange(seq_len, dtype=jnp.int32)).astype(F32)[:, None] * inv[None, :]
    cos, sin = jnp.cos(ang), jnp.sin(ang)
    return (jnp.tile(jnp.concatenate([cos, cos], axis=1), (1, MLA_HEADS)),
            jnp.tile(jnp.concatenate([-sin, sin], axis=1), (1, MLA_HEADS)))


def _pad_keys(new, cache, batch, seq_len, keys_padded):
    if cache is None and keys_padded == seq_len:
        return new
    w = new.shape[-1]
    parts = [new.reshape(batch, seq_len, w)]
    if cache is not None:
        parts.insert(0, cache.astype(new.dtype))
    have = sum(p.shape[1] for p in parts)
    if keys_padded > have:
        parts.append(jnp.zeros((batch, keys_padded - have, w), new.dtype))
    return jnp.concatenate(parts, axis=1).reshape(batch * keys_padded, w)


def _layer(x2, past_state, lw, layer_idx, lam_params, batch, seq_len, past, final_g, final):
    cache_ckv, cache_kr, cache_dk, cache_dv, h0, conv0 = past_state
    keys_on_rows = seq_len % LANES == 0
    tq = LANES if keys_on_rows else seq_len
    tk = 256 if keys_on_rows else 128
    keys_padded = -(-(past + seq_len) // tk) * tk
    cos_tab, sin_tab = _rope_tables(past, seq_len)
    (q_mla, kcat, ckvt, ckv, kr, sz_mla, qd, kdb, kd, vd, vdb, sz_diff, x_lru, sz_lru) = _proj_call(
        x2, lw, cos_tab, sin_tab, seq_len, keys_on_rows)

    cache_kcat = jnp.concatenate([cache_ckv, cache_kr], axis=-1) if past else None
    cache_dk2 = cache_dk.reshape(batch, past, DIFF_WIDTH) if past else None
    cache_dv2 = cache_dv.reshape(batch, past, DIFF_WIDTH) if past else None
    direct = not past and keys_padded == seq_len
    kcat_all = _pad_keys(kcat, cache_kcat, batch, seq_len, keys_padded)
    kd_rows = kdb if direct and keys_on_rows else _pad_keys(kd.astype(MXU_DTYPE), cache_dk2, batch, seq_len,
                                                            keys_padded)
    vd_rows = vdb if direct and not keys_on_rows else _pad_keys(vd.astype(MXU_DTYPE), cache_dv2, batch, seq_len,
                                                                keys_padded)

    lam_init = 0.8 - 0.6 * math.exp(-0.3 * layer_idx)
    lam_init_arr = jnp.array([lam_init], F32)
    slope = 2.0 ** (-8.0 * jnp.arange(1, DIFF_HEADS + 1, dtype=F32) / DIFF_HEADS)
    slopes = jnp.broadcast_to(slope.reshape(DIFF_PAIRS, 2, 1), (DIFF_PAIRS, 2, LANES))
    if keys_on_rows:
        ckvt_all = ckvt if direct else kcat_all[:, :MLA_KV_RANK].T
        vt_all = vdb if direct else vd_rows.T
        o_mla = _mla_kr_call(q_mla, kcat_all, ckvt_all, sz_mla, lw['w_uv_bd'].T, batch, seq_len, past,
                             keys_padded, tq, tk)
        o_diff = _diff_kr_call(lam_init_arr, lam_params, slopes, qd, kd_rows, vt_all, sz_diff, lw['subln2'],
                               batch, seq_len, past, keys_padded, tq, tk)
    else:
        kdt_all = kdb if direct else kd_rows.T
        o_mla = _mla_call(q_mla, kcat_all, sz_mla, lw['w_uv_bd'], batch, seq_len, past, keys_padded, tq, tk)
        o_diff = _diff_call(lam_init_arr, lam_params, slopes, qd, kdt_all, vd_rows, sz_diff, lw['subln2'],
                            batch, seq_len, past, keys_padded, tq, tk)

    o_lru, h_last, conv_new = _lru_call(x_lru, sz_lru, conv0, h0[:, None, :], lw, batch, seq_len, past)

    x_new = _merge_call(x2, o_mla, o_diff, o_lru, lw, final_g, final)
    states = (ckv.reshape(batch, seq_len, MLA_KV_RANK), kr.reshape(batch, seq_len, MLA_ROPE),
              kd.reshape(batch, seq_len, DIFF_HEADS, DIFF_V), vd.reshape(batch, seq_len, DIFF_HEADS, DIFF_V),
              h_last[:, 0, :], conv_new)
    return x_new, states


def kernel(x_prompt, x_sample, cache_mla_ckv, cache_mla_krope, cache_diff_k, cache_diff_v, state_lru_h, state_lru_conv, norm_g, w_in, mla_q_norm, mla_kv_norm, mla_w_uq, mla_w_uk, mla_w_uv, diff_lq1, diff_lk1, diff_lq2, diff_lk2, diff_subln, lru_conv_w, lru_conv_b, lru_w_a, lru_b_a, lru_w_x, lru_b_x, lru_lambda, w_o_mla, w_o_diff, w_o_lru, w_out, final_norm):
    depth = w_in.shape[0]
    bp, tp, _ = x_prompt.shape
    bs, ts, _ = x_sample.shape
    past = cache_mla_ckv.shape[2]
    zeros_h = jnp.zeros((bp, LRU_WIDTH), F32)
    zeros_conv = jnp.zeros((bp, CONV_W - 1, LRU_WIDTH), F32)
    xp = x_prompt.reshape(bp * tp, D_MODEL)
    xs = x_sample.reshape(bs * ts, D_MODEL)
    final_g = final_norm[None]
    p_states, s_states = [], []
    for l in range(depth):
        lw = _layer_weights(l, norm_g, w_in, mla_q_norm, mla_kv_norm, mla_w_uq, mla_w_uk, mla_w_uv,
                            diff_subln, lru_conv_w, lru_conv_b, lru_w_a, lru_b_a, lru_w_x, lru_b_x,
                            lru_lambda, w_o_mla, w_o_diff, w_o_lru, w_out)
        lam_params = jnp.stack([diff_lq1[l], diff_lk1[l], diff_lq2[l], diff_lk2[l]])
        final = l == depth - 1
        xp, st_p = _layer(xp, (None, None, None, None, zeros_h, zeros_conv), lw, l, lam_params, bp, tp, 0,
                          final_g, final)
        xs, st_s = _layer(xs, (cache_mla_ckv[l], cache_mla_krope[l], cache_diff_k[l], cache_diff_v[l],
                               state_lru_h[l], state_lru_conv[l]), lw, l, lam_params, bs, ts, past,
                          final_g, final)
        p_states.append(st_p)
        s_states.append(st_s)
    stack = lambda states, i: jnp.stack([s[i] for s in states], axis=0)
    return (xp.reshape(bp, tp, D_MODEL), xs.reshape(bs, ts, D_MODEL),
            *(stack(p_states, i) for i in range(6)), *(stack(s_states, i) for i in range(6)))
```

```python
import functools
import math

import jax
import jax.numpy as jnp
from jax import lax
from jax.experimental import pallas as pl
from jax.experimental.pallas import tpu as pltpu

F32 = jnp.float32
MXU_DTYPE = jnp.bfloat16

D_MODEL = 1024
CHUNK = 64
CHUNK_SHIFT = 6
EPS = 1e-6
NEG_INF = -1e30
LOG2E = math.log2(math.e)

MLA_HEADS = 8
MLA_Q_RANK = 256
MLA_KV_RANK = 128
MLA_NOPE = 64
MLA_ROPE = 32
MLA_V = 64
MLA_WIDTH = MLA_HEADS * MLA_V
MLA_QK = MLA_KV_RANK + MLA_ROPE
ROPE_BASE = 10000.0

DIFF_HEADS = 8
DIFF_D = 32
DIFF_V = 2 * DIFF_D
DIFF_WIDTH = DIFF_HEADS * DIFF_V
DIFF_PAIRS = DIFF_HEADS // 2
MLA_C1 = (MLA_NOPE + MLA_ROPE) ** -0.5 * LOG2E
DIFF_C1 = DIFF_D ** -0.5 * LOG2E

LRU_WIDTH = 512
LRU_BLOCKS = 8
CONV_W = 4
LRU_C = 8.0
N_BRANCH = 3

V7X_VMEM_BYTES = 64 * 1024 * 1024
LANES = 128
SUBLANES = 8
PACKED_ROWS = 16
V_ROWS = 2 * DIFF_V + PACKED_ROWS
VMEM_LIMIT = V7X_VMEM_BYTES * 3 // 4

_MAIN = (('c_q', MLA_Q_RANK), ('c_kv', MLA_KV_RANK), ('z_mla', MLA_WIDTH), ('q_d', DIFF_WIDTH),
         ('k_d', DIFF_WIDTH), ('v_d', DIFF_WIDTH), ('z_diff', DIFF_WIDTH), ('x_lru', LRU_WIDTH),
         ('z_lru', LRU_WIDTH))
_MAIN_OFF = {}
_o = 0
for _n, _w in _MAIN:
    _MAIN_OFF[_n] = (_o, _o + _w)
    _o += _w
MAIN_WIDTH = _o


def _params(sem):
    return pltpu.CompilerParams(dimension_semantics=sem, vmem_limit_bytes=VMEM_LIMIT)


def _rms(x, g):
    return x * lax.rsqrt(jnp.mean(x * x, axis=-1, keepdims=True) + EPS) * g


def _silu(z):
    return z * jax.nn.sigmoid(z)


def _mm(a, b):
    return jnp.dot(a.astype(MXU_DTYPE), b.astype(MXU_DTYPE), preferred_element_type=F32)


def _full(shape):
    return pl.BlockSpec(shape, lambda *_: (0,) * len(shape))


def _proj_kernel(x_ref, g_ref, wm_ref, wkr_ref, wkrs_ref, qg_ref, kvg_ref, wuqn_ref, wuqr_ref,
                 wuqrs_ref, wuk_ref, cos_ref, sin_ref,
                 q_ref, kcat_ref, ckvt_ref, ckv_ref, kr_ref, szm_ref, qd_ref, kdb_ref, kd_ref, vd_ref,
                 vdb_ref, szd_ref, xl_ref, szl_ref, *, keys_on_rows):
    xb = _rms(x_ref[...], g_ref[...]).astype(MXU_DTYPE)

    def proj(name):
        a, b = _MAIN_OFF[name]
        return jnp.dot(xb, wm_ref[:, a:b], preferred_element_type=F32)

    cos = cos_ref[...]
    sin = sin_ref[...]

    cq = _rms(proj('c_q'), qg_ref[...]).astype(MXU_DTYPE)
    q_nope = jnp.dot(cq, wuqn_ref[...], preferred_element_type=F32)
    q_rot = (jnp.dot(cq, wuqr_ref[...], preferred_element_type=F32) * cos
             + jnp.dot(cq, wuqrs_ref[...], preferred_element_type=F32) * sin)
    q_lat = jnp.dot(q_nope.astype(MXU_DTYPE), wuk_ref[...], preferred_element_type=F32)
    if keys_on_rows:
        q_lat_t = (q_lat * MLA_C1).T.astype(q_ref.dtype)
        q_rot_t = (q_rot * MLA_C1).T.astype(q_ref.dtype)
        for h in range(MLA_HEADS):
            q_ref[h, 0:MLA_KV_RANK, :] = q_lat_t[h * MLA_KV_RANK:(h + 1) * MLA_KV_RANK]
            q_ref[h, MLA_KV_RANK:MLA_QK, :] = q_rot_t[h * MLA_ROPE:(h + 1) * MLA_ROPE]
    else:
        for h in range(MLA_HEADS):
            q_ref[h, :, 0:MLA_KV_RANK] = q_lat[:, h * MLA_KV_RANK:(h + 1) * MLA_KV_RANK].astype(q_ref.dtype)
            q_ref[h, :, MLA_KV_RANK:MLA_QK] = q_rot[:, h * MLA_ROPE:(h + 1) * MLA_ROPE].astype(q_ref.dtype)

    ckv = _rms(proj('c_kv'), kvg_ref[...])
    kr = (jnp.dot(xb, wkr_ref[...], preferred_element_type=F32) * cos[:, :MLA_ROPE]
          + jnp.dot(xb, wkrs_ref[...], preferred_element_type=F32) * sin[:, :MLA_ROPE])
    ckv_ref[...] = ckv
    kr_ref[...] = kr
    kcat_ref[:, 0:MLA_KV_RANK] = ckv.astype(kcat_ref.dtype)
    kcat_ref[:, MLA_KV_RANK:MLA_QK] = kr.astype(kcat_ref.dtype)
    if keys_on_rows:
        ones_rows = jnp.where(lax.broadcasted_iota(jnp.int32, (PACKED_ROWS, x_ref.shape[0]), 0) == 0,
                              1.0, 0.0).astype(MXU_DTYPE)
        ckvt_ref[0:MLA_KV_RANK, :] = ckv.T.astype(ckvt_ref.dtype)
        ckvt_ref[MLA_KV_RANK:V_ROWS, :] = ones_rows
    else:
        ckvt_ref[...] = ckv.T.astype(ckvt_ref.dtype)
    szm_ref[...] = _silu(proj('z_mla'))

    qd = proj('q_d')
    kd = proj('k_d')
    vd = proj('v_d')
    kd_ref[...] = kd
    vd_ref[...] = vd
    if keys_on_rows:
        qd_ref[...] = (qd * DIFF_C1).T.astype(qd_ref.dtype)
        kdb_ref[...] = kd.astype(kdb_ref.dtype)
        vd_t = vd.T.astype(vdb_ref.dtype)
        pair_w = 2 * DIFF_V
        for p in range(DIFF_PAIRS):
            vdb_ref[p * V_ROWS:p * V_ROWS + pair_w, :] = vd_t[p * pair_w:(p + 1) * pair_w]
            vdb_ref[p * V_ROWS + pair_w:(p + 1) * V_ROWS, :] = ones_rows
    else:
        qd_ref[...] = qd.astype(qd_ref.dtype)
        kdb_ref[...] = kd.T.astype(kdb_ref.dtype)
        vdb_ref[...] = vd.astype(vdb_ref.dtype)
    szd_ref[...] = _silu(proj('z_diff'))

    xl_ref[...] = proj('x_lru')
    szl_ref[...] = _silu(proj('z_lru'))


def _proj_call(x2, lw, cos_tab, sin_tab, seq_len, keys_on_rows):
    n = x2.shape[0]
    tm = min(n, 256)
    assert n % tm == 0
    if seq_len % tm == 0:
        per = seq_len // tm
        tab_map = lambda i: (i % per, 0)
    else:
        assert tm % seq_len == 0
        reps = tm // seq_len
        cos_tab = jnp.tile(cos_tab, (reps, 1))
        sin_tab = jnp.tile(sin_tab, (reps, 1))
        tab_map = lambda i: (0, 0)
    row = lambda w: pl.BlockSpec((tm, w), lambda i: (i, 0))
    col = lambda w: pl.BlockSpec((w, tm), lambda i: (0, i))
    rope_w = MLA_HEADS * MLA_ROPE
    in_specs = [row(D_MODEL), _full((1, D_MODEL)), _full((D_MODEL, MAIN_WIDTH)),
                _full((D_MODEL, MLA_ROPE)), _full((D_MODEL, MLA_ROPE)),
                _full((1, MLA_Q_RANK)), _full((1, MLA_KV_RANK)),
                _full((MLA_Q_RANK, MLA_HEADS * MLA_NOPE)), _full((MLA_Q_RANK, rope_w)),
                _full((MLA_Q_RANK, rope_w)), _full((MLA_HEADS * MLA_NOPE, MLA_HEADS * MLA_KV_RANK)),
                pl.BlockSpec((tm, rope_w), tab_map), pl.BlockSpec((tm, rope_w), tab_map)]
    rows_f32 = lambda w: (jax.ShapeDtypeStruct((n, w), F32), row(w))
    rows_mxu = lambda w: (jax.ShapeDtypeStruct((n, w), MXU_DTYPE), row(w))
    cols_mxu = lambda w: (jax.ShapeDtypeStruct((w, n), MXU_DTYPE), col(w))
    if keys_on_rows:
        q_out = (jax.ShapeDtypeStruct((MLA_HEADS, MLA_QK, n), MXU_DTYPE),
                 pl.BlockSpec((MLA_HEADS, MLA_QK, tm), lambda i: (0, 0, i)))
        qd_out, kdb_out, vdb_out = cols_mxu(DIFF_WIDTH), rows_mxu(DIFF_WIDTH), cols_mxu(DIFF_PAIRS * V_ROWS)
    else:
        q_out = (jax.ShapeDtypeStruct((MLA_HEADS, n, MLA_QK), MXU_DTYPE),
                 pl.BlockSpec((MLA_HEADS, tm, MLA_QK), lambda i: (0, i, 0)))
        qd_out, kdb_out, vdb_out = rows_mxu(DIFF_WIDTH), cols_mxu(DIFF_WIDTH), rows_mxu(DIFF_WIDTH)
    outs = [q_out, rows_mxu(MLA_QK), cols_mxu(V_ROWS if keys_on_rows else MLA_KV_RANK),
            rows_f32(MLA_KV_RANK), rows_f32(MLA_ROPE),
            rows_f32(MLA_WIDTH), qd_out, kdb_out, rows_f32(DIFF_WIDTH), rows_f32(DIFF_WIDTH), vdb_out,
            rows_f32(DIFF_WIDTH), rows_f32(LRU_WIDTH), rows_f32(LRU_WIDTH)]
    return pl.pallas_call(
        functools.partial(_proj_kernel, keys_on_rows=keys_on_rows), grid=(n // tm,), in_specs=in_specs,
        out_specs=[o[1] for o in outs], out_shape=[o[0] for o in outs],
        compiler_params=_params(("parallel",)), name="proj",
    )(x2, lw['norm'], lw['w_main'], lw['w_kr'], lw['w_kr_sw'], lw['q_norm'], lw['kv_norm'],
      lw['w_uq_nope'], lw['w_uq_rope'], lw['w_uq_rope_sw'], lw['w_uk_bd'], cos_tab, sin_tab)


def _tile_counts(q0, tq, tk, total_keys):
    seen_by_all = jnp.minimum(((q0 >> CHUNK_SHIFT) + 1) * CHUNK, total_keys)
    seen_by_any = jnp.minimum((((q0 + tq - 1) >> CHUNK_SHIFT) + 1) * CHUNK, total_keys)
    return seen_by_all // tk, (seen_by_any + tk - 1) // tk


def _visible(q0, k0, tq, tk, total_keys):
    qpos = q0 + lax.broadcasted_iota(jnp.int32, (tq, tk), 0)
    kpos = k0 + lax.broadcasted_iota(jnp.int32, (tq, tk), 1)
    never = jnp.int32(jnp.iinfo(jnp.int32).max)
    return jnp.where(kpos < total_keys, kpos >> CHUNK_SHIFT, never) <= (qpos >> CHUNK_SHIFT)


def _visible_t(q0, k0, tq, tk, total_keys, width):
    qpos = q0 + lax.broadcasted_iota(jnp.int32, (tk, width), 1) % tq
    kpos = k0 + lax.broadcasted_iota(jnp.int32, (tk, width), 0)
    never = jnp.int32(jnp.iinfo(jnp.int32).max)
    return jnp.where(kpos < total_keys, kpos >> CHUNK_SHIFT, never) <= (qpos >> CHUNK_SHIFT)


def _mla_kernel(q_ref, k_ref, sz_ref, wuv_ref, o_ref, m_ref, l_ref, acc_ref, *, tq, tk, past, total_keys):
    i = pl.program_id(1)
    rows = MLA_HEADS * tq
    q = q_ref[...].reshape(rows, MLA_QK)
    scale = (MLA_NOPE + MLA_ROPE) ** -0.5
    m_ref[...] = jnp.full(m_ref.shape, NEG_INF, F32)
    l_ref[...] = jnp.zeros(l_ref.shape, F32)
    acc_ref[...] = jnp.zeros(acc_ref.shape, F32)
    q0 = past + i * tq
    n_full, n_any = _tile_counts(q0, tq, tk, total_keys)

    def step(j, masked):
        k0 = pl.multiple_of(j * tk, tk)
        k = k_ref[pl.ds(k0, tk), :]
        s = lax.dot_general(q, k, (((1,), (1,)), ((), ())), preferred_element_type=F32) * scale
        if masked:
            vis = _visible(q0, k0, tq, tk, total_keys)
            s = jnp.where(vis[None], s.reshape(MLA_HEADS, tq, tk), NEG_INF).reshape(rows, tk)
        m_prev = m_ref[...]
        m_new = jnp.maximum(m_prev, jnp.max(s, axis=-1, keepdims=True))
        p = jnp.exp(s - m_new)
        alpha = jnp.exp(m_prev - m_new)
        l_ref[...] = alpha * l_ref[...] + jnp.sum(p, axis=-1, keepdims=True)
        acc_ref[...] = alpha * acc_ref[...] + jnp.dot(p.astype(MXU_DTYPE), k[:, :MLA_KV_RANK],
                                                      preferred_element_type=F32)
        m_ref[...] = m_new

    def full_body(j, c):
        step(j, False)
        return c

    def edge_body(j, c):
        step(j, True)
        return c

    lax.fori_loop(0, n_full, full_body, 0)
    lax.fori_loop(n_full, n_any, edge_body, 0)

    o_lat = acc_ref[...] * (1.0 / l_ref[...])
    o_cat = jnp.concatenate([o_lat[h * tq:(h + 1) * tq] for h in range(MLA_HEADS)], axis=1)
    o_ref[...] = jnp.dot(o_cat.astype(MXU_DTYPE), wuv_ref[...], preferred_element_type=F32) * sz_ref[...]


def _mla_call(qcat, kcat_all, sz, w_uv_bd, batch, seq_len, past, keys_padded, tq, tk):
    nq = seq_len // tq
    rows = MLA_HEADS * tq
    kern = functools.partial(_mla_kernel, tq=tq, tk=tk, past=past, total_keys=past + seq_len)
    return pl.pallas_call(
        kern, grid=(batch, nq),
        in_specs=[pl.BlockSpec((MLA_HEADS, tq, MLA_QK), lambda b, i: (0, b * nq + i, 0)),
                  pl.BlockSpec((keys_padded, MLA_QK), lambda b, i: (b, 0)),
                  pl.BlockSpec((tq, MLA_WIDTH), lambda b, i: (b * nq + i, 0)),
                  _full((MLA_HEADS * MLA_KV_RANK, MLA_WIDTH))],
        out_specs=pl.BlockSpec((tq, MLA_WIDTH), lambda b, i: (b * nq + i, 0)),
        out_shape=jax.ShapeDtypeStruct((batch * seq_len, MLA_WIDTH), F32),
        scratch_shapes=[pltpu.VMEM((rows, 1), F32), pltpu.VMEM((rows, 1), F32),
                        pltpu.VMEM((rows, MLA_KV_RANK), F32)],
        compiler_params=_params(("parallel", "parallel")), name="mla",
    )(qcat, kcat_all, sz, w_uv_bd)


def _diff_kernel(lam_init_ref, lamp_ref, slope_ref, q_ref, kt_ref, v_ref, sz_ref, g_ref, o_ref,
                 m_ref, l_ref, acc_ref, *, tq, tk, past, total_keys):
    i = pl.program_id(2)
    n_maps = 4
    scale = DIFF_D ** -0.5
    qf = q_ref[...].astype(F32)
    qs = [qf[:, c * DIFF_D:(c + 1) * DIFF_D].astype(MXU_DTYPE) for c in range(n_maps)]
    slopes = [slope_ref[0:1, 0:1], slope_ref[1:2, 0:1]]
    m_ref[...] = jnp.full(m_ref.shape, NEG_INF, F32)
    l_ref[...] = jnp.zeros(l_ref.shape, F32)
    acc_ref[...] = jnp.zeros(acc_ref.shape, F32)
    q0 = past + i * tq
    n_full, n_any = _tile_counts(q0, tq, tk, total_keys)
    rel = (lax.broadcasted_iota(jnp.int32, (tq, tk), 0)
           - lax.broadcasted_iota(jnp.int32, (tq, tk), 1)).astype(F32)

    def step(j, masked):
        k0 = pl.multiple_of(j * tk, tk)
        kt = kt_ref[:, pl.ds(k0, tk)]
        v = v_ref[pl.ds(k0, tk), :]
        dist = jnp.abs(rel + (q0 - k0).astype(F32))
        bias = [slopes[0] * dist, slopes[1] * dist]
        if masked:
            vis = _visible(q0, k0, tq, tk, total_keys)
        ps, alphas = [], []
        for c in range(n_maps):
            s = jnp.dot(qs[c], kt[c * DIFF_D:(c + 1) * DIFF_D, :], preferred_element_type=F32) * scale
            s = s - bias[c // 2]
            if masked:
                s = jnp.where(vis, s, NEG_INF)
            m_prev = m_ref[c]
            m_new = jnp.maximum(m_prev, jnp.max(s, axis=-1, keepdims=True))
            p = jnp.exp(s - m_new)
            alpha = jnp.exp(m_prev - m_new)
            l_ref[c] = alpha * l_ref[c] + jnp.sum(p, axis=-1, keepdims=True)
            m_ref[c] = m_new
            ps.append(p.astype(MXU_DTYPE))
            alphas.append(alpha)
        pv = jnp.dot(jnp.concatenate(ps, axis=0), v, preferred_element_type=F32)
        acc_ref[...] = jnp.concatenate(alphas, axis=0) * acc_ref[...] + pv

    def full_body(j, c):
        step(j, False)
        return c

    def edge_body(j, c):
        step(j, True)
        return c

    lax.fori_loop(0, n_full, full_body, 0)
    lax.fori_loop(n_full, n_any, edge_body, 0)

    lam_init = lam_init_ref[0]
    lp = lamp_ref[...]
    lam = (jnp.exp(jnp.sum(lp[0:1] * lp[1:2], axis=-1, keepdims=True))
           - jnp.exp(jnp.sum(lp[2:3] * lp[3:4], axis=-1, keepdims=True)) + lam_init)
    outs = []
    for h in range(2):
        a0 = acc_ref[(2 * h) * tq:(2 * h + 1) * tq, :] * (1.0 / l_ref[2 * h])
        a1 = acc_ref[(2 * h + 1) * tq:(2 * h + 2) * tq, :] * (1.0 / l_ref[2 * h + 1])
        outs.append(a0 - lam * a1)
    lane = lax.broadcasted_iota(jnp.int32, (tq, 2 * DIFF_V), 1)
    first = lane < DIFF_V
    o = jnp.where(first, outs[0], outs[1])
    sq = o * o
    ms0 = jnp.sum(jnp.where(first, sq, 0.0), axis=-1, keepdims=True)
    ms1 = jnp.sum(jnp.where(first, 0.0, sq), axis=-1, keepdims=True)
    ms = jnp.where(first, ms0, ms1) * (1.0 / DIFF_V)
    y = o * lax.rsqrt(ms + EPS) * g_ref[...]
    o_ref[...] = y * (1.0 - lam_init) * sz_ref[...]


def _diff_call(lam_init_arr, lam_params, slopes, qd, kdt_all, v_all, sz, subln2, batch, seq_len, past,
               keys_padded, tq, tk):
    nq = seq_len // tq
    pair_w = 2 * DIFF_V
    kern = functools.partial(_diff_kernel, tq=tq, tk=tk, past=past, total_keys=past + seq_len)
    return pl.pallas_call(
        kern, grid=(batch, DIFF_PAIRS, nq),
        in_specs=[pl.BlockSpec(memory_space=pltpu.SMEM),
                  _full((4, DIFF_D)),
                  pl.BlockSpec((None, 2, LANES), lambda b, p, i: (p, 0, 0)),
                  pl.BlockSpec((tq, pair_w), lambda b, p, i: (b * nq + i, p)),
                  pl.BlockSpec((pair_w, keys_padded), lambda b, p, i: (p, b)),
                  pl.BlockSpec((keys_padded, pair_w), lambda b, p, i: (b, p)),
                  pl.BlockSpec((tq, pair_w), lambda b, p, i: (b * nq + i, p)),
                  _full((1, pair_w))],
        out_specs=pl.BlockSpec((tq, pair_w), lambda b, p, i: (b * nq + i, p)),
        out_shape=jax.ShapeDtypeStruct((batch * seq_len, DIFF_WIDTH), F32),
        scratch_shapes=[pltpu.VMEM((4, tq, 1), F32), pltpu.VMEM((4, tq, 1), F32),
                        pltpu.VMEM((4 * tq, pair_w), F32)],
        compiler_params=_params(("parallel", "parallel", "parallel")), name="diff",
    )(lam_init_arr, lam_params, slopes, qd, kdt_all, v_all, sz, subln2)


def _pipeline_scratch(tk, cols):
    return [pltpu.VMEM((tk, cols), F32), pltpu.VMEM((tk, cols), F32),
            pltpu.VMEM((tk, cols), MXU_DTYPE), pltpu.VMEM((tk, cols), MXU_DTYPE),
            pltpu.VMEM((1, cols), F32), pltpu.VMEM((1, cols), F32)]


def _pipelined_tiles(n, scores, softmax, accumulate, s_refs, p_refs, a_refs):
    last = jnp.maximum(n - 1, 0)
    p_refs[1][...] = jnp.zeros(p_refs[1].shape, p_refs[1].dtype)
    a_refs[1][...] = jnp.ones(a_refs[1].shape, F32)
    s_refs[0][...] = scores(0)

    def half(a, cur, nxt):
        accumulate(jnp.maximum(a - 1, 0), p_refs[nxt][...], a_refs[nxt][...])
        s_refs[nxt][...] = scores(jnp.minimum(a + 1, last))
        p_refs[cur][...], a_refs[cur][...] = softmax(a, s_refs[cur][...])

    def body(t, c):
        a = 2 * t
        half(a, 0, 1)

        @pl.when(a + 1 < n)
        def _():
            half(a + 1, 1, 0)
        return c

    lax.fori_loop(0, (n + 1) // 2, body, 0)
    odd = n % 2 == 1

    @pl.when(odd)
    def _():
        accumulate(last, p_refs[0][...], a_refs[0][...])

    @pl.when(jnp.logical_not(odd))
    def _():
        accumulate(last, p_refs[1][...], a_refs[1][...])


def _mla_kr_kernel(q_ref, k_ref, vt_ref, sz_ref, wuvt_ref, o_ref, m_ref, acc_ref, s0_ref, s1_ref,
                   p0_ref, p1_ref, a0_ref, a1_ref, *, tq, tk, past, total_keys):
    i = pl.program_id(1)
    s_refs, p_refs, a_refs = (s0_ref, s1_ref), (p0_ref, p1_ref), (a0_ref, a1_ref)
    cols = MLA_HEADS * tq
    m_ref[...] = jnp.full(m_ref.shape, NEG_INF, F32)
    acc_ref[...] = jnp.zeros(acc_ref.shape, F32)
    q0 = past + i * tq
    n_full, n_any = _tile_counts(q0, tq, tk, total_keys)

    def scores(j):
        k0 = pl.multiple_of(j * tk, tk)
        q_all = jnp.concatenate([q_ref[h] for h in range(MLA_HEADS)], axis=1)
        return jnp.dot(k_ref[pl.ds(k0, tk), :], q_all, preferred_element_type=F32)

    def accumulate(j, p, alpha):
        k0 = pl.multiple_of(j * tk, tk)
        pv = jnp.dot(vt_ref[:, pl.ds(k0, tk)], p, preferred_element_type=F32)
        acc_ref[...] = alpha * acc_ref[...] + pv

    def softmax(j, s, masked):
        if masked:
            s = jnp.where(_visible_t(q0, j * tk, tq, tk, total_keys, cols), s, NEG_INF)
        m_prev = m_ref[...]
        m_new = jnp.maximum(m_prev, jnp.max(s, axis=0, keepdims=True))
        m_ref[...] = m_new
        return jnp.exp2(s - m_new).astype(MXU_DTYPE), jnp.exp2(m_prev - m_new)

    _pipelined_tiles(n_full, scores, lambda j, s: softmax(j, s, False), accumulate, s_refs, p_refs, a_refs)

    def edge_body(j, c):
        accumulate(j, *softmax(j, scores(j), True))
        return c

    lax.fori_loop(n_full, n_any, edge_body, 0)

    o_t = acc_ref[0:MLA_KV_RANK, :] * (1.0 / acc_ref[MLA_KV_RANK:MLA_KV_RANK + 1, :])
    o_cat_t = jnp.concatenate([o_t[:, h * tq:(h + 1) * tq] for h in range(MLA_HEADS)], axis=0)
    out_t = jnp.dot(wuvt_ref[...], o_cat_t.astype(MXU_DTYPE), preferred_element_type=F32)
    o_ref[...] = out_t.T * sz_ref[...]


def _mla_kr_call(q_t, kcat_all, ckvt_all, sz, w_uv_bd_t, batch, seq_len, past, keys_padded, tq, tk):
    nq = seq_len // tq
    cols = MLA_HEADS * tq
    kern = functools.partial(_mla_kr_kernel, tq=tq, tk=tk, past=past, total_keys=past + seq_len)
    return pl.pallas_call(
        kern, grid=(batch, nq),
        in_specs=[pl.BlockSpec((MLA_HEADS, MLA_QK, tq), lambda b, i: (0, 0, b * nq + i)),
                  pl.BlockSpec((keys_padded, MLA_QK), lambda b, i: (b, 0)),
                  pl.BlockSpec((V_ROWS, keys_padded), lambda b, i: (0, b)),
                  pl.BlockSpec((tq, MLA_WIDTH), lambda b, i: (b * nq + i, 0)),
                  _full((MLA_WIDTH, MLA_HEADS * MLA_KV_RANK))],
        out_specs=pl.BlockSpec((tq, MLA_WIDTH), lambda b, i: (b * nq + i, 0)),
        out_shape=jax.ShapeDtypeStruct((batch * seq_len, MLA_WIDTH), F32),
        scratch_shapes=[pltpu.VMEM((1, cols), F32), pltpu.VMEM((V_ROWS, cols), F32)]
        + _pipeline_scratch(tk, cols),
        compiler_params=_params(("parallel", "parallel")), name="mla_kr",
    )(q_t, kcat_all, ckvt_all, sz, w_uv_bd_t)


def _diff_kr_kernel(lam_init_ref, lamp_ref, slope_ref, qt_ref, k_ref, vt_ref, sz_ref, g_ref, o_ref,
                    m_ref, acc_ref, s0_ref, s1_ref, p0_ref, p1_ref, a0_ref, a1_ref, *, tq, tk, past,
                    total_keys):
    i = pl.program_id(2)
    s_refs, p_refs, a_refs = (s0_ref, s1_ref), (p0_ref, p1_ref), (a0_ref, a1_ref)
    n_maps = 4
    pair_w = 2 * DIFF_V
    qt = qt_ref[...].astype(F32)
    feat_map = lax.broadcasted_iota(jnp.int32, (pair_w, tq), 0) // DIFF_D
    w_all = jnp.concatenate([jnp.where(feat_map == c, qt, 0.0) for c in range(n_maps)],
                            axis=1).astype(MXU_DTYPE)
    slope2 = [slope_ref[0:1, 0:1] * LOG2E, slope_ref[1:2, 0:1] * LOG2E]
    m_ref[...] = jnp.full(m_ref.shape, NEG_INF, F32)
    acc_ref[...] = jnp.zeros(acc_ref.shape, F32)
    q0 = past + i * tq
    n_full, n_any = _tile_counts(q0, tq, tk, total_keys)
    rel = (lax.broadcasted_iota(jnp.int32, (tk, tq), 0)
           - lax.broadcasted_iota(jnp.int32, (tk, tq), 1)).astype(F32)

    def scores(j):
        k0 = pl.multiple_of(j * tk, tk)
        return jnp.dot(k_ref[pl.ds(k0, tk), :], w_all, preferred_element_type=F32)

    def accumulate(j, p_all, alpha_all):
        k0 = pl.multiple_of(j * tk, tk)
        pv = jnp.dot(vt_ref[:, pl.ds(k0, tk)], p_all, preferred_element_type=F32)
        acc_ref[...] = alpha_all * acc_ref[...] + pv

    def softmax(j, s_all, masked):
        k0 = j * tk
        dist = jnp.abs(rel + (k0 - q0).astype(F32))
        bias = [slope2[0] * dist, slope2[1] * dist]
        if masked:
            vis = _visible_t(q0, k0, tq, tk, total_keys, tq)
        ps, alphas = [], []
        for c in range(n_maps):
            lanes = slice(c * tq, (c + 1) * tq)
            u = s_all[:, lanes] - bias[c // 2]
            if masked:
                u = jnp.where(vis, u, NEG_INF)
            m_prev = m_ref[:, lanes]
            m_new = jnp.maximum(m_prev, jnp.max(u, axis=0, keepdims=True))
            m_ref[:, lanes] = m_new
            ps.append(jnp.exp2(u - m_new).astype(MXU_DTYPE))
            alphas.append(jnp.exp2(m_prev - m_new))
        return jnp.concatenate(ps, axis=1), jnp.concatenate(alphas, axis=1)

    _pipelined_tiles(n_full, scores, lambda j, s: softmax(j, s, False), accumulate, s_refs, p_refs, a_refs)

    def edge_body(j, c):
        accumulate(j, *softmax(j, scores(j), True))
        return c

    lax.fori_loop(n_full, n_any, edge_body, 0)

    lam_init = lam_init_ref[0]
    lp = lamp_ref[...]
    lam = (jnp.exp(jnp.sum(lp[0:1] * lp[1:2], axis=-1, keepdims=True))
           - jnp.exp(jnp.sum(lp[2:3] * lp[3:4], axis=-1, keepdims=True)) + lam_init)
    a = acc_ref[0:pair_w, :] * (1.0 / acc_ref[pair_w:pair_w + 1, :])
    halves = []
    for h in range(2):
        rows = slice(h * DIFF_V, (h + 1) * DIFF_V)
        halves.append(a[rows, (2 * h) * tq:(2 * h + 1) * tq] - lam * a[rows, (2 * h + 1) * tq:(2 * h + 2) * tq])
    o = jnp.concatenate(halves, axis=0).T
    lane = lax.broadcasted_iota(jnp.int32, (tq, pair_w), 1)
    first = lane < DIFF_V
    sq = o * o
    ms0 = jnp.sum(jnp.where(first, sq, 0.0), axis=-1, keepdims=True)
    ms1 = jnp.sum(jnp.where(first, 0.0, sq), axis=-1, keepdims=True)
    ms = jnp.where(first, ms0, ms1) * (1.0 / DIFF_V)
    y = o * lax.rsqrt(ms + EPS) * g_ref[...]
    o_ref[...] = y * (1.0 - lam_init) * sz_ref[...]


def _diff_kr_call(lam_init_arr, lam_params, slopes, qd_t, k_all, vt_all, sz, subln2, batch, seq_len, past,
                  keys_padded, tq, tk):
    nq = seq_len // tq
    pair_w = 2 * DIFF_V
    kern = functools.partial(_diff_kr_kernel, tq=tq, tk=tk, past=past, total_keys=past + seq_len)
    return pl.pallas_call(
        kern, grid=(batch, DIFF_PAIRS, nq),
        in_specs=[pl.BlockSpec(memory_space=pltpu.SMEM),
                  _full((4, DIFF_D)),
                  pl.BlockSpec((None, 2, LANES), lambda b, p, i: (p, 0, 0)),
                  pl.BlockSpec((pair_w, tq), lambda b, p, i: (p, b * nq + i)),
                  pl.BlockSpec((keys_padded, pair_w), lambda b, p, i: (b, p)),
                  pl.BlockSpec((V_ROWS, keys_padded), lambda b, p, i: (p, b)),
                  pl.BlockSpec((tq, pair_w), lambda b, p, i: (b * nq + i, p)),
                  _full((1, pair_w))],
        out_specs=pl.BlockSpec((tq, pair_w), lambda b, p, i: (b * nq + i, p)),
        out_shape=jax.ShapeDtypeStruct((batch * seq_len, DIFF_WIDTH), F32),
        scratch_shapes=[pltpu.VMEM((1, 4 * tq), F32), pltpu.VMEM((V_ROWS, 4 * tq), F32)]
        + _pipeline_scratch(tk, 4 * tq),
        compiler_params=_params(("parallel", "parallel", "parallel")), name="diff_kr",
    )(lam_init_arr, lam_params, slopes, qd_t, k_all, vt_all, sz, subln2)


_EXT_PAD = SUBLANES


def _lru_kernel(x_ref, sz_ref, conv0_ref, h0_ref, cw_ref, cb_ref, wax_ref, ba_ref, bx_ref, lam_ref,
                o_ref, hlast_ref, convnew_ref, ext_ref, a_ref, b_ref, hs_ref, hc_ref, *, tt, past):
    i = pl.program_id(1)
    hist = CONV_W - 1

    @pl.when(i == 0)
    def _():
        hc_ref[...] = h0_ref[...]
        ext_ref[_EXT_PAD - hist:_EXT_PAD, :] = conv0_ref[...]

    ext_ref[_EXT_PAD:_EXT_PAD + tt, :] = x_ref[...]
    cw = cw_ref[...]
    xc = cb_ref[...]
    for k in range(CONV_W):
        xc = xc + ext_ref[_EXT_PAD - hist + k:_EXT_PAD - hist + k + tt, :] * cw[k:k + 1, :]
    gates = jnp.dot(xc.astype(MXU_DTYPE), wax_ref[...], preferred_element_type=F32)
    r = jax.nn.sigmoid(gates[:, :LRU_WIDTH] + ba_ref[...])
    gi = jax.nn.sigmoid(gates[:, LRU_WIDTH:] + bx_ref[...])
    neg_lam = -lam_ref[...]
    softplus = jnp.maximum(neg_lam, 0.0) + jnp.log1p(jnp.exp(-jnp.abs(neg_lam)))
    log_a = -LRU_C * r * softplus
    a = jnp.exp(log_a)
    qpos = past + i * tt + lax.broadcasted_iota(jnp.int32, (tt, LRU_WIDTH), 0)
    mult = jnp.where(qpos == 0, 1.0, jnp.sqrt(1.0 - a * a))
    a_ref[...] = a
    b_ref[...] = mult * gi * xc

    def row(t, h):
        h = a_ref[pl.ds(t, 1), :] * h + b_ref[pl.ds(t, 1), :]
        hs_ref[pl.ds(t, 1), :] = h
        return h

    h = lax.fori_loop(0, tt, row, hc_ref[...], unroll=8)
    hc_ref[...] = h
    o_ref[...] = hs_ref[...] * sz_ref[...]
    hlast_ref[...] = h
    tail = ext_ref[_EXT_PAD + tt - hist:_EXT_PAD + tt, :]
    convnew_ref[...] = tail
    ext_ref[_EXT_PAD - hist:_EXT_PAD, :] = tail


def _lru_call(x_lru, sz, conv0, h0, lw, batch, seq_len, past):
    tt = min(seq_len, 512)
    assert seq_len % tt == 0 and tt >= CONV_W - 1
    nt = seq_len // tt
    hist = CONV_W - 1
    kern = functools.partial(_lru_kernel, tt=tt, past=past)
    tile = pl.BlockSpec((tt, LRU_WIDTH), lambda b, i: (b * nt + i, 0))
    return pl.pallas_call(
        kern, grid=(batch, nt),
        in_specs=[tile, tile,
                  pl.BlockSpec((None, hist, LRU_WIDTH), lambda b, i: (b, 0, 0)),
                  pl.BlockSpec((None, 1, LRU_WIDTH), lambda b, i: (b, 0, 0)),
                  _full((CONV_W, LRU_WIDTH)), _full((1, LRU_WIDTH)),
                  _full((LRU_WIDTH, 2 * LRU_WIDTH)), _full((1, LRU_WIDTH)), _full((1, LRU_WIDTH)),
                  _full((1, LRU_WIDTH))],
        out_specs=[tile,
                   pl.BlockSpec((None, 1, LRU_WIDTH), lambda b, i: (b, 0, 0)),
                   pl.BlockSpec((None, hist, LRU_WIDTH), lambda b, i: (b, 0, 0))],
        out_shape=[jax.ShapeDtypeStruct((batch * seq_len, LRU_WIDTH), F32),
                   jax.ShapeDtypeStruct((batch, 1, LRU_WIDTH), F32),
                   jax.ShapeDtypeStruct((batch, hist, LRU_WIDTH), F32)],
        scratch_shapes=[pltpu.VMEM((_EXT_PAD + tt, LRU_WIDTH), F32), pltpu.VMEM((tt, LRU_WIDTH), F32),
                        pltpu.VMEM((tt, LRU_WIDTH), F32), pltpu.VMEM((tt, LRU_WIDTH), F32),
                        pltpu.VMEM((1, LRU_WIDTH), F32)],
        compiler_params=_params(("parallel", "arbitrary")), name="lru",
    )(x_lru, sz, conv0, h0, lw['conv_w'], lw['conv_b'], lw['w_ax_bd'], lw['b_a'], lw['b_x'], lw['lam'])


def _merge_kernel(x_ref, g_ref, wg_ref, om_ref, od_ref, ol_ref, wom_ref, wod_ref, wol_ref, wout_ref,
                  fg_ref, o_ref, *, final):
    x = x_ref[...]
    xb = _rms(x, g_ref[...]).astype(MXU_DTYPE)
    merged = None
    for b, (br_ref, w_ref) in enumerate(((om_ref, wom_ref), (od_ref, wod_ref), (ol_ref, wol_ref))):
        gate = jax.nn.sigmoid(jnp.dot(xb, wg_ref[:, b * D_MODEL:(b + 1) * D_MODEL],
                                      preferred_element_type=F32))
        term = gate * jnp.dot(br_ref[...].astype(MXU_DTYPE), w_ref[...], preferred_element_type=F32)
        merged = term if merged is None else merged + term
    y = x + jnp.dot(merged.astype(MXU_DTYPE), wout_ref[...], preferred_element_type=F32)
    if final:
        y = _rms(y, fg_ref[...])
    o_ref[...] = y


def _merge_call(x2, o_mla, o_diff, o_lru, lw, final_g, final):
    n = x2.shape[0]
    tm = min(n, 256)
    row = lambda w: pl.BlockSpec((tm, w), lambda i: (i, 0))
    return pl.pallas_call(
        functools.partial(_merge_kernel, final=final), grid=(n // tm,),
        in_specs=[row(D_MODEL), _full((1, D_MODEL)), _full((D_MODEL, N_BRANCH * D_MODEL)),
                  row(MLA_WIDTH), row(DIFF_WIDTH), row(LRU_WIDTH),
                  _full((MLA_WIDTH, D_MODEL)), _full((DIFF_WIDTH, D_MODEL)), _full((LRU_WIDTH, D_MODEL)),
                  _full((D_MODEL, D_MODEL)), _full((1, D_MODEL))],
        out_specs=row(D_MODEL),
        out_shape=jax.ShapeDtypeStruct((n, D_MODEL), F32),
        compiler_params=_params(("parallel",)), name="merge",
    )(x2, lw['norm'], lw['w_gate'], o_mla, o_diff, o_lru, lw['w_o_mla'], lw['w_o_diff'], lw['w_o_lru'],
      lw['w_out'], final_g)


def _block_diag(blocks):
    n, r, c = blocks.shape
    eye = jnp.eye(n, dtype=blocks.dtype)
    return (blocks[:, :, None, :] * eye[:, None, :, None]).reshape(n * r, n * c)


def _swap_halves(w, groups):
    rows, cols = w.shape
    return w.reshape(rows, groups, 2, cols // groups // 2)[:, :, ::-1, :].reshape(rows, cols)


def _layer_weights(l, norm_g, w_in, mla_q_norm, mla_kv_norm, mla_w_uq, mla_w_uk, mla_w_uv, diff_subln,
                   lru_conv_w, lru_conv_b, lru_w_a, lru_b_a, lru_w_x, lru_b_x, lru_lambda, w_o_mla,
                   w_o_diff, w_o_lru, w_out):
    cast = lambda w: w.astype(MXU_DTYPE)
    names = ('c_q', 'c_kv', 'k_r', 'z_mla', 'q_d', 'k_d', 'v_d', 'z_diff', 'x_lru', 'z_lru', 'gate')
    widths = (MLA_Q_RANK, MLA_KV_RANK, MLA_ROPE, MLA_WIDTH, DIFF_WIDTH, DIFF_WIDTH, DIFF_WIDTH, DIFF_WIDTH,
              LRU_WIDTH, LRU_WIDTH, N_BRANCH * D_MODEL)
    cols, off = {}, 0
    for name, w in zip(names, widths):
        cols[name] = w_in[l][:, off:off + w]
        off += w
    uq = mla_w_uq[l].reshape(MLA_Q_RANK, MLA_HEADS, MLA_NOPE + MLA_ROPE)
    uq_rope = uq[:, :, MLA_NOPE:].reshape(MLA_Q_RANK, MLA_HEADS * MLA_ROPE)
    w_kr = cols['k_r']
    return {
        'norm': norm_g[l][None],
        'w_main': cast(jnp.concatenate([cols[n] for n, _ in _MAIN], axis=1)),
        'w_kr': cast(w_kr), 'w_kr_sw': cast(_swap_halves(w_kr, 1)),
        'w_gate': cast(cols['gate']),
        'q_norm': mla_q_norm[l][None], 'kv_norm': mla_kv_norm[l][None],
        'w_uq_nope': cast(uq[:, :, :MLA_NOPE].reshape(MLA_Q_RANK, MLA_HEADS * MLA_NOPE)),
        'w_uq_rope': cast(uq_rope), 'w_uq_rope_sw': cast(_swap_halves(uq_rope, MLA_HEADS)),
        'w_uk_bd': cast(_block_diag(mla_w_uk[l].transpose(1, 2, 0))),
        'w_uv_bd': cast(_block_diag(mla_w_uv[l].transpose(1, 0, 2))),
        'subln2': jnp.tile(diff_subln[l], 2)[None],
        'conv_w': lru_conv_w[l], 'conv_b': lru_conv_b[l][None],
        'w_ax_bd': cast(jnp.concatenate([_block_diag(lru_w_a[l]), _block_diag(lru_w_x[l])], axis=1)),
        'b_a': lru_b_a[l][None], 'b_x': lru_b_x[l][None], 'lam': lru_lambda[l][None],
        'w_o_mla': cast(w_o_mla[l]), 'w_o_diff': cast(w_o_diff[l]), 'w_o_lru': cast(w_o_lru[l]),
        'w_out': cast(w_out[l]),
    }


def _rope_tables(past, seq_len):
    half = MLA_ROPE // 2
    inv = ROPE_BASE ** (-jnp.arange(half, dtype=F32) / half)
    ang = (past + jnp.arange(seq_len, dtype=jnp.int32)).astype(F32)[:, None] * inv[None, :]
    cos, sin = jnp.cos(ang), jnp.sin(ang)
    return (jnp.tile(jnp.concatenate([cos, cos], axis=1), (1, MLA_HEADS)),
            jnp.tile(jnp.concatenate([-sin, sin], axis=1), (1, MLA_HEADS)))


def _pad_keys(new, cache, batch, seq_len, keys_padded):
    if cache is None and keys_padded == seq_len:
        return new
    w = new.shape[-1]
    parts = [new.reshape(batch, seq_len, w)]
    if cache is not None:
        parts.insert(0, cache.astype(new.dtype))
    have = sum(p.shape[1] for p in parts)
    if keys_padded > have:
        parts.append(jnp.zeros((batch, keys_padded - have, w), new.dtype))
    return jnp.concatenate(parts, axis=1).reshape(batch * keys_padded, w)


def _with_ones_rows(vt, groups):
    n = vt.shape[1]
    w = vt.shape[0] // groups
    extra = jnp.zeros((groups, V_ROWS - w, n), vt.dtype).at[:, 0, :].set(1)
    return jnp.concatenate([vt.reshape(groups, w, n), extra], axis=1).reshape(groups * V_ROWS, n)


def _layer(x2, past_state, lw, layer_idx, lam_params, batch, seq_len, past, final_g, final):
    cache_ckv, cache_kr, cache_dk, cache_dv, h0, conv0 = past_state
    keys_on_rows = seq_len % LANES == 0
    tq = (256 if seq_len % 256 == 0 else LANES) if keys_on_rows else seq_len
    tk = 256 if keys_on_rows else 128
    keys_padded = -(-(past + seq_len) // tk) * tk
    cos_tab, sin_tab = _rope_tables(past, seq_len)
    (q_mla, kcat, ckvt, ckv, kr, sz_mla, qd, kdb, kd, vd, vdb, sz_diff, x_lru, sz_lru) = _proj_call(
        x2, lw, cos_tab, sin_tab, seq_len, keys_on_rows)

    cache_kcat = jnp.concatenate([cache_ckv, cache_kr], axis=-1) if past else None
    cache_dk2 = cache_dk.reshape(batch, past, DIFF_WIDTH) if past else None
    cache_dv2 = cache_dv.reshape(batch, past, DIFF_WIDTH) if past else None
    direct = not past and keys_padded == seq_len
    kcat_all = _pad_keys(kcat, cache_kcat, batch, seq_len, keys_padded)
    kd_rows = kdb if direct and keys_on_rows else _pad_keys(kd.astype(MXU_DTYPE), cache_dk2, batch, seq_len,
                                                            keys_padded)
    vd_rows = vdb if direct and not keys_on_rows else _pad_keys(vd.astype(MXU_DTYPE), cache_dv2, batch, seq_len,
                                                                keys_padded)

    lam_init = 0.8 - 0.6 * math.exp(-0.3 * layer_idx)
    lam_init_arr = jnp.array([lam_init], F32)
    slope = 2.0 ** (-8.0 * jnp.arange(1, DIFF_HEADS + 1, dtype=F32) / DIFF_HEADS)
    slopes = jnp.broadcast_to(slope.reshape(DIFF_PAIRS, 2, 1), (DIFF_PAIRS, 2, LANES))
    if keys_on_rows:
        if direct:
            ckvt_all, vt_all = ckvt, vdb
        else:
            ckvt_all = _with_ones_rows(kcat_all[:, :MLA_KV_RANK].T, 1)
            vt_all = _with_ones_rows(vd_rows.T, DIFF_PAIRS)
        o_mla = _mla_kr_call(q_mla, kcat_all, ckvt_all, sz_mla, lw['w_uv_bd'].T, batch, seq_len, past,
                             keys_padded, tq, tk)
        o_diff = _diff_kr_call(lam_init_arr, lam_params, slopes, qd, kd_rows, vt_all, sz_diff, lw['subln2'],
                               batch, seq_len, past, keys_padded, tq, tk)
    else:
        kdt_all = kdb if direct else kd_rows.T
        o_mla = _mla_call(q_mla, kcat_all, sz_mla, lw['w_uv_bd'], batch, seq_len, past, keys_padded, tq, tk)
        o_diff = _diff_call(lam_init_arr, lam_params, slopes, qd, kdt_all, vd_rows, sz_diff, lw['subln2'],
                            batch, seq_len, past, keys_padded, tq, tk)

    o_lru, h_last, conv_new = _lru_call(x_lru, sz_lru, conv0, h0[:, None, :], lw, batch, seq_len, past)

    x_new = _merge_call(x2, o_mla, o_diff, o_lru, lw, final_g, final)
    states = (ckv.reshape(batch, seq_len, MLA_KV_RANK), kr.reshape(batch, seq_len, MLA_ROPE),
              kd.reshape(batch, seq_len, DIFF_HEADS, DIFF_V), vd.reshape(batch, seq_len, DIFF_HEADS, DIFF_V),
              h_last[:, 0, :], conv_new)
    return x_new, states


def kernel(x_prompt, x_sample, cache_mla_ckv, cache_mla_krope, cache_diff_k, cache_diff_v, state_lru_h, state_lru_conv, norm_g, w_in, mla_q_norm, mla_kv_norm, mla_w_uq, mla_w_uk, mla_w_uv, diff_lq1, diff_lk1, diff_lq2, diff_lk2, diff_subln, lru_conv_w, lru_conv_b, lru_w_a, lru_b_a, lru_w_x, lru_b_x, lru_lambda, w_o_mla, w_o_diff, w_o_lru, w_out, final_norm):
    depth = w_in.shape[0]
    bp, tp, _ = x_prompt.shape
    bs, ts, _ = x_sample.shape
    past = cache_mla_ckv.shape[2]
    zeros_h = jnp.zeros((bp, LRU_WIDTH), F32)
    zeros_conv = jnp.zeros((bp, CONV_W - 1, LRU_WIDTH), F32)
    xp = x_prompt.reshape(bp * tp, D_MODEL)
    xs = x_sample.reshape(bs * ts, D_MODEL)
    final_g = final_norm[None]
    p_states, s_states = [], []
    for l in range(depth):
        lw = _layer_weights(l, norm_g, w_in, mla_q_norm, mla_kv_norm, mla_w_uq, mla_w_uk, mla_w_uv,
                            diff_subln, lru_conv_w, lru_conv_b, lru_w_a, lru_b_a, lru_w_x, lru_b_x,
                            lru_lambda, w_o_mla, w_o_diff, w_o_lru, w_out)
        lam_params = jnp.stack([diff_lq1[l], diff_lk1[l], diff_lq2[l], diff_lk2[l]])
        final = l == depth - 1
        xp, st_p = _layer(xp, (None, None, None, None, zeros_h, zeros_conv), lw, l, lam_params, bp, tp, 0,
                          final_g, final)
        xs, st_s = _layer(xs, (cache_mla_ckv[l], cache_mla_krope[l], cache_diff_k[l], cache_diff_v[l],
                               state_lru_h[l], state_lru_conv[l]), lw, l, lam_params, bs, ts, past,
                          final_g, final)
        p_states.append(st_p)
        s_states.append(st_s)
    stack = lambda states, i: jnp.stack([s[i] for s in states], axis=0)
    return (xp.reshape(bp, tp, D_MODEL), xs.reshape(bs, ts, D_MODEL),
            *(stack(p_states, i) for i in range(6)), *(stack(s_states, i) for i in range(6)))
```

```python
import functools
import math

import jax
import jax.numpy as jnp
from jax import lax
from jax.experimental import pallas as pl
from jax.experimental.pallas import tpu as pltpu

F32 = jnp.float32
MXU_DTYPE = jnp.bfloat16

D_MODEL = 1024
CHUNK = 64
CHUNK_SHIFT = 6
EPS = 1e-6
NEG_INF = -1e30
LOG2E = math.log2(math.e)

MLA_HEADS = 8
MLA_Q_RANK = 256
MLA_KV_RANK = 128
MLA_NOPE = 64
MLA_ROPE = 32
MLA_V = 64
MLA_WIDTH = MLA_HEADS * MLA_V
MLA_QK = MLA_KV_RANK + MLA_ROPE
ROPE_BASE = 10000.0

DIFF_HEADS = 8
DIFF_D = 32
DIFF_V = 2 * DIFF_D
DIFF_WIDTH = DIFF_HEADS * DIFF_V
DIFF_PAIRS = DIFF_HEADS // 2
MLA_C1 = (MLA_NOPE + MLA_ROPE) ** -0.5 * LOG2E
DIFF_C1 = DIFF_D ** -0.5 * LOG2E

LRU_WIDTH = 512
LRU_BLOCKS = 8
CONV_W = 4
LRU_C = 8.0
N_BRANCH = 3

V7X_VMEM_BYTES = 64 * 1024 * 1024
LANES = 128
SUBLANES = 8
PACKED_ROWS = 16
V_ROWS = 2 * DIFF_V + PACKED_ROWS
VMEM_LIMIT = V7X_VMEM_BYTES * 3 // 4

_MAIN = (('c_q', MLA_Q_RANK), ('c_kv', MLA_KV_RANK), ('z_mla', MLA_WIDTH), ('q_d', DIFF_WIDTH),
         ('k_d', DIFF_WIDTH), ('v_d', DIFF_WIDTH), ('z_diff', DIFF_WIDTH), ('x_lru', LRU_WIDTH),
         ('z_lru', LRU_WIDTH))
_MAIN_OFF = {}
_o = 0
for _n, _w in _MAIN:
    _MAIN_OFF[_n] = (_o, _o + _w)
    _o += _w
MAIN_WIDTH = _o


def _params(sem):
    return pltpu.CompilerParams(dimension_semantics=sem, vmem_limit_bytes=VMEM_LIMIT)


def _rms(x, g):
    return x * lax.rsqrt(jnp.mean(x * x, axis=-1, keepdims=True) + EPS) * g


def _silu(z):
    return z * jax.nn.sigmoid(z)


def _mm(a, b):
    return jnp.dot(a.astype(MXU_DTYPE), b.astype(MXU_DTYPE), preferred_element_type=F32)


def _full(shape):
    return pl.BlockSpec(shape, lambda *_: (0,) * len(shape))


_PROJ_STATE_OUTS = ('ckv', 'kr', 'sz_mla', 'kd', 'vd', 'sz_diff', 'x_lru', 'sz_lru')
_PROJ_KR_OUTS = ('q', 'qd', 'kcat', 'ckv_t', 'kd_rows', 'vd_t') + _PROJ_STATE_OUTS
_PROJ_CACHE_OUTS = ('q', 'qd') + _PROJ_STATE_OUTS


def _proj_kernel(x_ref, g_ref, wm_ref, wkr_ref, wkrs_ref, qg_ref, kvg_ref, wuqn_ref, wuqr_ref,
                 wuqrs_ref, wuk_ref, cos_ref, sin_ref, *out_refs, keys_on_rows):
    o = dict(zip(_PROJ_KR_OUTS if keys_on_rows else _PROJ_CACHE_OUTS, out_refs))
    q_ref, qd_ref = o['q'], o['qd']
    xb = _rms(x_ref[...], g_ref[...]).astype(MXU_DTYPE)

    def proj(name):
        a, b = _MAIN_OFF[name]
        return jnp.dot(xb, wm_ref[:, a:b], preferred_element_type=F32)

    cos = cos_ref[...]
    sin = sin_ref[...]

    cq = _rms(proj('c_q'), qg_ref[...]).astype(MXU_DTYPE)
    q_nope = jnp.dot(cq, wuqn_ref[...], preferred_element_type=F32)
    q_rot = (jnp.dot(cq, wuqr_ref[...], preferred_element_type=F32) * cos
             + jnp.dot(cq, wuqrs_ref[...], preferred_element_type=F32) * sin)
    q_lat = jnp.dot(q_nope.astype(MXU_DTYPE), wuk_ref[...], preferred_element_type=F32)
    if keys_on_rows:
        q_lat_t = (q_lat * MLA_C1).T.astype(q_ref.dtype)
        q_rot_t = (q_rot * MLA_C1).T.astype(q_ref.dtype)
        for h in range(MLA_HEADS):
            q_ref[h, 0:MLA_KV_RANK, :] = q_lat_t[h * MLA_KV_RANK:(h + 1) * MLA_KV_RANK]
            q_ref[h, MLA_KV_RANK:MLA_QK, :] = q_rot_t[h * MLA_ROPE:(h + 1) * MLA_ROPE]
    else:
        for h in range(MLA_HEADS):
            q_ref[h, :, 0:MLA_KV_RANK] = q_lat[:, h * MLA_KV_RANK:(h + 1) * MLA_KV_RANK].astype(q_ref.dtype)
            q_ref[h, :, MLA_KV_RANK:MLA_QK] = q_rot[:, h * MLA_ROPE:(h + 1) * MLA_ROPE].astype(q_ref.dtype)

    ckv = _rms(proj('c_kv'), kvg_ref[...])
    kr = (jnp.dot(xb, wkr_ref[...], preferred_element_type=F32) * cos[:, :MLA_ROPE]
          + jnp.dot(xb, wkrs_ref[...], preferred_element_type=F32) * sin[:, :MLA_ROPE])
    o['ckv'][...] = ckv
    o['kr'][...] = kr
    if keys_on_rows:
        ones_rows = jnp.where(lax.broadcasted_iota(jnp.int32, (PACKED_ROWS, x_ref.shape[0]), 0) == 0,
                              1.0, 0.0).astype(MXU_DTYPE)
        o['kcat'][:, 0:MLA_KV_RANK] = ckv.astype(MXU_DTYPE)
        o['kcat'][:, MLA_KV_RANK:MLA_QK] = kr.astype(MXU_DTYPE)
        o['ckv_t'][0:MLA_KV_RANK, :] = ckv.T.astype(MXU_DTYPE)
        o['ckv_t'][MLA_KV_RANK:V_ROWS, :] = ones_rows
    o['sz_mla'][...] = _silu(proj('z_mla'))

    qd = proj('q_d')
    kd = proj('k_d')
    vd = proj('v_d')
    o['kd'][...] = kd
    o['vd'][...] = vd
    if keys_on_rows:
        qd_ref[...] = (qd * DIFF_C1).T.astype(qd_ref.dtype)
        o['kd_rows'][...] = kd.astype(MXU_DTYPE)
        vd_t = vd.T.astype(MXU_DTYPE)
        pair_w = 2 * DIFF_V
        for p in range(DIFF_PAIRS):
            o['vd_t'][p * V_ROWS:p * V_ROWS + pair_w, :] = vd_t[p * pair_w:(p + 1) * pair_w]
            o['vd_t'][p * V_ROWS + pair_w:(p + 1) * V_ROWS, :] = ones_rows
    else:
        qd_ref[...] = qd.astype(qd_ref.dtype)
    o['sz_diff'][...] = _silu(proj('z_diff'))

    o['x_lru'][...] = proj('x_lru')
    o['sz_lru'][...] = _silu(proj('z_lru'))


def _proj_call(x2, lw, cos_tab, sin_tab, seq_len, keys_on_rows):
    n = x2.shape[0]
    tm = min(n, 256)
    assert n % tm == 0
    if seq_len % tm == 0:
        per = seq_len // tm
        tab_map = lambda i: (i % per, 0)
    else:
        assert tm % seq_len == 0
        reps = tm // seq_len
        cos_tab = jnp.tile(cos_tab, (reps, 1))
        sin_tab = jnp.tile(sin_tab, (reps, 1))
        tab_map = lambda i: (0, 0)
    row = lambda w: pl.BlockSpec((tm, w), lambda i: (i, 0))
    col = lambda w: pl.BlockSpec((w, tm), lambda i: (0, i))
    rope_w = MLA_HEADS * MLA_ROPE
    in_specs = [row(D_MODEL), _full((1, D_MODEL)), _full((D_MODEL, MAIN_WIDTH)),
                _full((D_MODEL, MLA_ROPE)), _full((D_MODEL, MLA_ROPE)),
                _full((1, MLA_Q_RANK)), _full((1, MLA_KV_RANK)),
                _full((MLA_Q_RANK, MLA_HEADS * MLA_NOPE)), _full((MLA_Q_RANK, rope_w)),
                _full((MLA_Q_RANK, rope_w)), _full((MLA_HEADS * MLA_NOPE, MLA_HEADS * MLA_KV_RANK)),
                pl.BlockSpec((tm, rope_w), tab_map), pl.BlockSpec((tm, rope_w), tab_map)]
    rows_f32 = lambda w: (jax.ShapeDtypeStruct((n, w), F32), row(w))
    rows_mxu = lambda w: (jax.ShapeDtypeStruct((n, w), MXU_DTYPE), row(w))
    cols_mxu = lambda w: (jax.ShapeDtypeStruct((w, n), MXU_DTYPE), col(w))
    outs = {'ckv': rows_f32(MLA_KV_RANK), 'kr': rows_f32(MLA_ROPE), 'sz_mla': rows_f32(MLA_WIDTH),
            'kd': rows_f32(DIFF_WIDTH), 'vd': rows_f32(DIFF_WIDTH), 'sz_diff': rows_f32(DIFF_WIDTH),
            'x_lru': rows_f32(LRU_WIDTH), 'sz_lru': rows_f32(LRU_WIDTH)}
    if keys_on_rows:
        outs.update({'q': (jax.ShapeDtypeStruct((MLA_HEADS, MLA_QK, n), MXU_DTYPE),
                           pl.BlockSpec((MLA_HEADS, MLA_QK, tm), lambda i: (0, 0, i))),
                     'qd': cols_mxu(DIFF_WIDTH), 'kcat': rows_mxu(MLA_QK), 'ckv_t': cols_mxu(V_ROWS),
                     'kd_rows': rows_mxu(DIFF_WIDTH), 'vd_t': cols_mxu(DIFF_PAIRS * V_ROWS)})
    else:
        outs.update({'q': (jax.ShapeDtypeStruct((MLA_HEADS, n, MLA_QK), MXU_DTYPE),
                           pl.BlockSpec((MLA_HEADS, tm, MLA_QK), lambda i: (0, i, 0))),
                     'qd': rows_mxu(DIFF_WIDTH)})
    names = _PROJ_KR_OUTS if keys_on_rows else _PROJ_CACHE_OUTS
    arrays = pl.pallas_call(
        functools.partial(_proj_kernel, keys_on_rows=keys_on_rows), grid=(n // tm,), in_specs=in_specs,
        out_specs=[outs[k][1] for k in names], out_shape=[outs[k][0] for k in names],
        compiler_params=_params(("parallel",)), name="proj",
    )(x2, lw['norm'], lw['w_main'], lw['w_kr'], lw['w_kr_sw'], lw['q_norm'], lw['kv_norm'],
      lw['w_uq_nope'], lw['w_uq_rope'], lw['w_uq_rope_sw'], lw['w_uk_bd'], cos_tab, sin_tab)
    return dict(zip(names, arrays))


def _tile_counts(q0, tq, tk, total_keys):
    seen_by_all = jnp.minimum(((q0 >> CHUNK_SHIFT) + 1) * CHUNK, total_keys)
    seen_by_any = jnp.minimum((((q0 + tq - 1) >> CHUNK_SHIFT) + 1) * CHUNK, total_keys)
    return seen_by_all // tk, (seen_by_any + tk - 1) // tk


def _visible(q0, k0, tq, tk, total_keys):
    qpos = q0 + lax.broadcasted_iota(jnp.int32, (tq, tk), 0)
    kpos = k0 + lax.broadcasted_iota(jnp.int32, (tq, tk), 1)
    never = jnp.int32(jnp.iinfo(jnp.int32).max)
    return jnp.where(kpos < total_keys, kpos >> CHUNK_SHIFT, never) <= (qpos >> CHUNK_SHIFT)


def _visible_t(q0, k0, tq, tk, total_keys, width):
    qpos = q0 + lax.broadcasted_iota(jnp.int32, (tk, width), 1) % tq
    kpos = k0 + lax.broadcasted_iota(jnp.int32, (tk, width), 0)
    never = jnp.int32(jnp.iinfo(jnp.int32).max)
    return jnp.where(kpos < total_keys, kpos >> CHUNK_SHIFT, never) <= (qpos >> CHUNK_SHIFT)


def _mla_kernel(q_ref, k_ref, sz_ref, wuv_ref, o_ref, m_ref, l_ref, acc_ref, *, tq, tk, past, total_keys):
    i = pl.program_id(1)
    rows = MLA_HEADS * tq
    q = q_ref[...].reshape(rows, MLA_QK)
    scale = (MLA_NOPE + MLA_ROPE) ** -0.5
    m_ref[...] = jnp.full(m_ref.shape, NEG_INF, F32)
    l_ref[...] = jnp.zeros(l_ref.shape, F32)
    acc_ref[...] = jnp.zeros(acc_ref.shape, F32)
    q0 = past + i * tq
    n_full, n_any = _tile_counts(q0, tq, tk, total_keys)

    def step(j, masked):
        k0 = pl.multiple_of(j * tk, tk)
        k = k_ref[pl.ds(k0, tk), :]
        s = lax.dot_general(q, k, (((1,), (1,)), ((), ())), preferred_element_type=F32) * scale
        if masked:
            vis = _visible(q0, k0, tq, tk, total_keys)
            s = jnp.where(vis[None], s.reshape(MLA_HEADS, tq, tk), NEG_INF).reshape(rows, tk)
        m_prev = m_ref[...]
        m_new = jnp.maximum(m_prev, jnp.max(s, axis=-1, keepdims=True))
        p = jnp.exp(s - m_new)
        alpha = jnp.exp(m_prev - m_new)
        l_ref[...] = alpha * l_ref[...] + jnp.sum(p, axis=-1, keepdims=True)
        acc_ref[...] = alpha * acc_ref[...] + jnp.dot(p.astype(MXU_DTYPE), k[:, :MLA_KV_RANK],
                                                      preferred_element_type=F32)
        m_ref[...] = m_new

    def full_body(j, c):
        step(j, False)
        return c

    def edge_body(j, c):
        step(j, True)
        return c

    lax.fori_loop(0, n_full, full_body, 0)
    lax.fori_loop(n_full, n_any, edge_body, 0)

    o_lat = acc_ref[...] * (1.0 / l_ref[...])
    o_cat = jnp.concatenate([o_lat[h * tq:(h + 1) * tq] for h in range(MLA_HEADS)], axis=1)
    o_ref[...] = jnp.dot(o_cat.astype(MXU_DTYPE), wuv_ref[...], preferred_element_type=F32) * sz_ref[...]


def _mla_call(qcat, kcat_all, sz, w_uv_bd, batch, seq_len, past, keys_padded, tq, tk):
    nq = seq_len // tq
    rows = MLA_HEADS * tq
    kern = functools.partial(_mla_kernel, tq=tq, tk=tk, past=past, total_keys=past + seq_len)
    return pl.pallas_call(
        kern, grid=(batch, nq),
        in_specs=[pl.BlockSpec((MLA_HEADS, tq, MLA_QK), lambda b, i: (0, b * nq + i, 0)),
                  pl.BlockSpec((keys_padded, MLA_QK), lambda b, i: (b, 0)),
                  pl.BlockSpec((tq, MLA_WIDTH), lambda b, i: (b * nq + i, 0)),
                  _full((MLA_HEADS * MLA_KV_RANK, MLA_WIDTH))],
        out_specs=pl.BlockSpec((tq, MLA_WIDTH), lambda b, i: (b * nq + i, 0)),
        out_shape=jax.ShapeDtypeStruct((batch * seq_len, MLA_WIDTH), F32),
        scratch_shapes=[pltpu.VMEM((rows, 1), F32), pltpu.VMEM((rows, 1), F32),
                        pltpu.VMEM((rows, MLA_KV_RANK), F32)],
        compiler_params=_params(("parallel", "parallel")), name="mla",
    )(qcat, kcat_all, sz, w_uv_bd)


def _diff_kernel(lam_init_ref, lamp_ref, slope_ref, q_ref, kt_ref, v_ref, sz_ref, g_ref, o_ref,
                 m_ref, l_ref, acc_ref, *, tq, tk, past, total_keys):
    i = pl.program_id(2)
    n_maps = 4
    scale = DIFF_D ** -0.5
    qf = q_ref[...].astype(F32)
    qs = [qf[:, c * DIFF_D:(c + 1) * DIFF_D].astype(MXU_DTYPE) for c in range(n_maps)]
    slopes = [slope_ref[0:1, 0:1], slope_ref[1:2, 0:1]]
    m_ref[...] = jnp.full(m_ref.shape, NEG_INF, F32)
    l_ref[...] = jnp.zeros(l_ref.shape, F32)
    acc_ref[...] = jnp.zeros(acc_ref.shape, F32)
    q0 = past + i * tq
    n_full, n_any = _tile_counts(q0, tq, tk, total_keys)
    rel = (lax.broadcasted_iota(jnp.int32, (tq, tk), 0)
           - lax.broadcasted_iota(jnp.int32, (tq, tk), 1)).astype(F32)

    def step(j, masked):
        k0 = pl.multiple_of(j * tk, tk)
        kt = kt_ref[:, pl.ds(k0, tk)]
        v = v_ref[pl.ds(k0, tk), :]
        dist = jnp.abs(rel + (q0 - k0).astype(F32))
        bias = [slopes[0] * dist, slopes[1] * dist]
        if masked:
            vis = _visible(q0, k0, tq, tk, total_keys)
        ps, alphas = [], []
        for c in range(n_maps):
            s = jnp.dot(qs[c], kt[c * DIFF_D:(c + 1) * DIFF_D, :], preferred_element_type=F32) * scale
            s = s - bias[c // 2]
            if masked:
                s = jnp.where(vis, s, NEG_INF)
            m_prev = m_ref[c]
            m_new = jnp.maximum(m_prev, jnp.max(s, axis=-1, keepdims=True))
            p = jnp.exp(s - m_new)
            alpha = jnp.exp(m_prev - m_new)
            l_ref[c] = alpha * l_ref[c] + jnp.sum(p, axis=-1, keepdims=True)
            m_ref[c] = m_new
            ps.append(p.astype(MXU_DTYPE))
            alphas.append(alpha)
        pv = jnp.dot(jnp.concatenate(ps, axis=0), v, preferred_element_type=F32)
        acc_ref[...] = jnp.concatenate(alphas, axis=0) * acc_ref[...] + pv

    def full_body(j, c):
        step(j, False)
        return c

    def edge_body(j, c):
        step(j, True)
        return c

    lax.fori_loop(0, n_full, full_body, 0)
    lax.fori_loop(n_full, n_any, edge_body, 0)

    lam_init = lam_init_ref[0]
    lp = lamp_ref[...]
    lam = (jnp.exp(jnp.sum(lp[0:1] * lp[1:2], axis=-1, keepdims=True))
           - jnp.exp(jnp.sum(lp[2:3] * lp[3:4], axis=-1, keepdims=True)) + lam_init)
    outs = []
    for h in range(2):
        a0 = acc_ref[(2 * h) * tq:(2 * h + 1) * tq, :] * (1.0 / l_ref[2 * h])
        a1 = acc_ref[(2 * h + 1) * tq:(2 * h + 2) * tq, :] * (1.0 / l_ref[2 * h + 1])
        outs.append(a0 - lam * a1)
    lane = lax.broadcasted_iota(jnp.int32, (tq, 2 * DIFF_V), 1)
    first = lane < DIFF_V
    o = jnp.where(first, outs[0], outs[1])
    sq = o * o
    ms0 = jnp.sum(jnp.where(first, sq, 0.0), axis=-1, keepdims=True)
    ms1 = jnp.sum(jnp.where(first, 0.0, sq), axis=-1, keepdims=True)
    ms = jnp.where(first, ms0, ms1) * (1.0 / DIFF_V)
    y = o * lax.rsqrt(ms + EPS) * g_ref[...]
    o_ref[...] = y * (1.0 - lam_init) * sz_ref[...]


def _diff_call(lam_init_arr, lam_params, slopes, qd, kdt_all, v_all, sz, subln2, batch, seq_len, past,
               keys_padded, tq, tk):
    nq = seq_len // tq
    pair_w = 2 * DIFF_V
    kern = functools.partial(_diff_kernel, tq=tq, tk=tk, past=past, total_keys=past + seq_len)
    return pl.pallas_call(
        kern, grid=(batch, DIFF_PAIRS, nq),
        in_specs=[pl.BlockSpec(memory_space=pltpu.SMEM),
                  _full((4, DIFF_D)),
                  pl.BlockSpec((None, 2, LANES), lambda b, p, i: (p, 0, 0)),
                  pl.BlockSpec((tq, pair_w), lambda b, p, i: (b * nq + i, p)),
                  pl.BlockSpec((pair_w, keys_padded), lambda b, p, i: (p, b)),
                  pl.BlockSpec((keys_padded, pair_w), lambda b, p, i: (b, p)),
                  pl.BlockSpec((tq, pair_w), lambda b, p, i: (b * nq + i, p)),
                  _full((1, pair_w))],
        out_specs=pl.BlockSpec((tq, pair_w), lambda b, p, i: (b * nq + i, p)),
        out_shape=jax.ShapeDtypeStruct((batch * seq_len, DIFF_WIDTH), F32),
        scratch_shapes=[pltpu.VMEM((4, tq, 1), F32), pltpu.VMEM((4, tq, 1), F32),
                        pltpu.VMEM((4 * tq, pair_w), F32)],
        compiler_params=_params(("parallel", "parallel", "parallel")), name="diff",
    )(lam_init_arr, lam_params, slopes, qd, kdt_all, v_all, sz, subln2)


def _pipeline_scratch(tk, cols):
    return [pltpu.VMEM((tk, cols), F32), pltpu.VMEM((tk, cols), F32),
            pltpu.VMEM((tk, cols), MXU_DTYPE), pltpu.VMEM((tk, cols), MXU_DTYPE),
            pltpu.VMEM((1, cols), F32), pltpu.VMEM((1, cols), F32)]


def _pipelined_tiles(n, scores, softmax, accumulate, s_refs, p_refs, a_refs):
    last = jnp.maximum(n - 1, 0)
    p_refs[1][...] = jnp.zeros(p_refs[1].shape, p_refs[1].dtype)
    a_refs[1][...] = jnp.ones(a_refs[1].shape, F32)
    s_refs[0][...] = scores(0)

    def half(a, cur, nxt):
        accumulate(jnp.maximum(a - 1, 0), p_refs[nxt][...], a_refs[nxt][...])
        s_refs[nxt][...] = scores(jnp.minimum(a + 1, last))
        p_refs[cur][...], a_refs[cur][...] = softmax(a, s_refs[cur][...])

    def body(t, c):
        a = 2 * t
        half(a, 0, 1)

        @pl.when(a + 1 < n)
        def _():
            half(a + 1, 1, 0)
        return c

    lax.fori_loop(0, (n + 1) // 2, body, 0)
    odd = n % 2 == 1

    @pl.when(odd)
    def _():
        accumulate(last, p_refs[0][...], a_refs[0][...])

    @pl.when(jnp.logical_not(odd))
    def _():
        accumulate(last, p_refs[1][...], a_refs[1][...])


def _mla_kr_kernel(q_ref, k_ref, vt_ref, sz_ref, wuvt_ref, o_ref, m_ref, acc_ref, s0_ref, s1_ref,
                   p0_ref, p1_ref, a0_ref, a1_ref, *, tq, tk, past, total_keys):
    i = pl.program_id(1)
    s_refs, p_refs, a_refs = (s0_ref, s1_ref), (p0_ref, p1_ref), (a0_ref, a1_ref)
    cols = MLA_HEADS * tq
    m_ref[...] = jnp.full(m_ref.shape, NEG_INF, F32)
    acc_ref[...] = jnp.zeros(acc_ref.shape, F32)
    q0 = past + i * tq
    n_full, n_any = _tile_counts(q0, tq, tk, total_keys)

    def scores(j):
        k0 = pl.multiple_of(j * tk, tk)
        q_all = jnp.concatenate([q_ref[h] for h in range(MLA_HEADS)], axis=1)
        return jnp.dot(k_ref[pl.ds(k0, tk), :], q_all, preferred_element_type=F32)

    def accumulate(j, p, alpha):
        k0 = pl.multiple_of(j * tk, tk)
        pv = jnp.dot(vt_ref[:, pl.ds(k0, tk)], p, preferred_element_type=F32)
        acc_ref[...] = alpha * acc_ref[...] + pv

    def softmax(j, s, masked):
        if masked:
            s = jnp.where(_visible_t(q0, j * tk, tq, tk, total_keys, cols), s, NEG_INF)
        m_prev = m_ref[...]
        m_new = jnp.maximum(m_prev, jnp.max(s, axis=0, keepdims=True))
        m_ref[...] = m_new
        return jnp.exp2(s - m_new).astype(MXU_DTYPE), jnp.exp2(m_prev - m_new)

    _pipelined_tiles(n_full, scores, lambda j, s: softmax(j, s, False), accumulate, s_refs, p_refs, a_refs)

    def edge_body(j, c):
        accumulate(j, *softmax(j, scores(j), True))
        return c

    lax.fori_loop(n_full, n_any, edge_body, 0)

    o_t = acc_ref[0:MLA_KV_RANK, :] * (1.0 / acc_ref[MLA_KV_RANK:MLA_KV_RANK + 1, :])
    o_cat_t = jnp.concatenate([o_t[:, h * tq:(h + 1) * tq] for h in range(MLA_HEADS)], axis=0)
    out_t = jnp.dot(wuvt_ref[...], o_cat_t.astype(MXU_DTYPE), preferred_element_type=F32)
    o_ref[...] = out_t.T * sz_ref[...]


def _mla_kr_call(q_t, kcat_all, ckvt_all, sz, w_uv_bd_t, batch, seq_len, past, keys_padded, tq, tk):
    nq = seq_len // tq
    cols = MLA_HEADS * tq
    kern = functools.partial(_mla_kr_kernel, tq=tq, tk=tk, past=past, total_keys=past + seq_len)
    return pl.pallas_call(
        kern, grid=(batch, nq),
        in_specs=[pl.BlockSpec((MLA_HEADS, MLA_QK, tq), lambda b, i: (0, 0, b * nq + i)),
                  pl.BlockSpec((keys_padded, MLA_QK), lambda b, i: (b, 0)),
                  pl.BlockSpec((V_ROWS, keys_padded), lambda b, i: (0, b)),
                  pl.BlockSpec((tq, MLA_WIDTH), lambda b, i: (b * nq + i, 0)),
                  _full((MLA_WIDTH, MLA_HEADS * MLA_KV_RANK))],
        out_specs=pl.BlockSpec((tq, MLA_WIDTH), lambda b, i: (b * nq + i, 0)),
        out_shape=jax.ShapeDtypeStruct((batch * seq_len, MLA_WIDTH), F32),
        scratch_shapes=[pltpu.VMEM((1, cols), F32), pltpu.VMEM((V_ROWS, cols), F32)]
        + _pipeline_scratch(tk, cols),
        compiler_params=_params(("parallel", "parallel")), name="mla_kr",
    )(q_t, kcat_all, ckvt_all, sz, w_uv_bd_t)


def _diff_kr_kernel(lam_init_ref, lamp_ref, slope_ref, qt_ref, k_ref, vt_ref, sz_ref, g_ref, o_ref,
                    m_ref, acc_ref, s0_ref, s1_ref, p0_ref, p1_ref, a0_ref, a1_ref, *, tq, tk, past,
                    total_keys):
    i = pl.program_id(2)
    s_refs, p_refs, a_refs = (s0_ref, s1_ref), (p0_ref, p1_ref), (a0_ref, a1_ref)
    n_maps = 4
    pair_w = 2 * DIFF_V
    qt = qt_ref[...].astype(F32)
    feat_map = lax.broadcasted_iota(jnp.int32, (pair_w, tq), 0) // DIFF_D
    w_all = jnp.concatenate([jnp.where(feat_map == c, qt, 0.0) for c in range(n_maps)],
                            axis=1).astype(MXU_DTYPE)
    slope2 = [slope_ref[0:1, 0:1] * LOG2E, slope_ref[1:2, 0:1] * LOG2E]
    m_ref[...] = jnp.full(m_ref.shape, NEG_INF, F32)
    acc_ref[...] = jnp.zeros(acc_ref.shape, F32)
    q0 = past + i * tq
    n_full, n_any = _tile_counts(q0, tq, tk, total_keys)
    rel = (lax.broadcasted_iota(jnp.int32, (tk, tq), 0)
           - lax.broadcasted_iota(jnp.int32, (tk, tq), 1)).astype(F32)

    def scores(j):
        k0 = pl.multiple_of(j * tk, tk)
        return jnp.dot(k_ref[pl.ds(k0, tk), :], w_all, preferred_element_type=F32)

    def accumulate(j, p_all, alpha_all):
        k0 = pl.multiple_of(j * tk, tk)
        pv = jnp.dot(vt_ref[:, pl.ds(k0, tk)], p_all, preferred_element_type=F32)
        acc_ref[...] = alpha_all * acc_ref[...] + pv

    def softmax(j, s_all, masked):
        k0 = j * tk
        dist = jnp.abs(rel + (k0 - q0).astype(F32))
        bias = [slope2[0] * dist, slope2[1] * dist]
        if masked:
            vis = _visible_t(q0, k0, tq, tk, total_keys, tq)
        ps, alphas = [], []
        for c in range(n_maps):
            lanes = slice(c * tq, (c + 1) * tq)
            u = s_all[:, lanes] - bias[c // 2]
            if masked:
                u = jnp.where(vis, u, NEG_INF)
            m_prev = m_ref[:, lanes]
            m_new = jnp.maximum(m_prev, jnp.max(u, axis=0, keepdims=True))
            m_ref[:, lanes] = m_new
            ps.append(jnp.exp2(u - m_new).astype(MXU_DTYPE))
            alphas.append(jnp.exp2(m_prev - m_new))
        return jnp.concatenate(ps, axis=1), jnp.concatenate(alphas, axis=1)

    _pipelined_tiles(n_full, scores, lambda j, s: softmax(j, s, False), accumulate, s_refs, p_refs, a_refs)

    def edge_body(j, c):
        accumulate(j, *softmax(j, scores(j), True))
        return c

    lax.fori_loop(n_full, n_any, edge_body, 0)

    lam_init = lam_init_ref[0]
    lp = lamp_ref[...]
    lam = (jnp.exp(jnp.sum(lp[0:1] * lp[1:2], axis=-1, keepdims=True))
           - jnp.exp(jnp.sum(lp[2:3] * lp[3:4], axis=-1, keepdims=True)) + lam_init)
    a = acc_ref[0:pair_w, :] * (1.0 / acc_ref[pair_w:pair_w + 1, :])
    halves = []
    for h in range(2):
        rows = slice(h * DIFF_V, (h + 1) * DIFF_V)
        halves.append(a[rows, (2 * h) * tq:(2 * h + 1) * tq] - lam * a[rows, (2 * h + 1) * tq:(2 * h + 2) * tq])
    o = jnp.concatenate(halves, axis=0).T
    lane = lax.broadcasted_iota(jnp.int32, (tq, pair_w), 1)
    first = lane < DIFF_V
    sq = o * o
    ms0 = jnp.sum(jnp.where(first, sq, 0.0), axis=-1, keepdims=True)
    ms1 = jnp.sum(jnp.where(first, 0.0, sq), axis=-1, keepdims=True)
    ms = jnp.where(first, ms0, ms1) * (1.0 / DIFF_V)
    y = o * lax.rsqrt(ms + EPS) * g_ref[...]
    o_ref[...] = y * (1.0 - lam_init) * sz_ref[...]


def _diff_kr_call(lam_init_arr, lam_params, slopes, qd_t, k_all, vt_all, sz, subln2, batch, seq_len, past,
                  keys_padded, tq, tk):
    nq = seq_len // tq
    pair_w = 2 * DIFF_V
    kern = functools.partial(_diff_kr_kernel, tq=tq, tk=tk, past=past, total_keys=past + seq_len)
    return pl.pallas_call(
        kern, grid=(batch, DIFF_PAIRS, nq),
        in_specs=[pl.BlockSpec(memory_space=pltpu.SMEM),
                  _full((4, DIFF_D)),
                  pl.BlockSpec((None, 2, LANES), lambda b, p, i: (p, 0, 0)),
                  pl.BlockSpec((pair_w, tq), lambda b, p, i: (p, b * nq + i)),
                  pl.BlockSpec((keys_padded, pair_w), lambda b, p, i: (b, p)),
                  pl.BlockSpec((V_ROWS, keys_padded), lambda b, p, i: (p, b)),
                  pl.BlockSpec((tq, pair_w), lambda b, p, i: (b * nq + i, p)),
                  _full((1, pair_w))],
        out_specs=pl.BlockSpec((tq, pair_w), lambda b, p, i: (b * nq + i, p)),
        out_shape=jax.ShapeDtypeStruct((batch * seq_len, DIFF_WIDTH), F32),
        scratch_shapes=[pltpu.VMEM((1, 4 * tq), F32), pltpu.VMEM((V_ROWS, 4 * tq), F32)]
        + _pipeline_scratch(tk, 4 * tq),
        compiler_params=_params(("parallel", "parallel", "parallel")), name="diff_kr",
    )(lam_init_arr, lam_params, slopes, qd_t, k_all, vt_all, sz, subln2)


CACHE_TILE = 512


def _dot_nt(a, b):
    return lax.dot_general(a, b, (((1,), (1,)), ((), ())), preferred_element_type=F32)


def _online_softmax_t(u, carry, v_t):
    m, l, acc = carry
    m_new = jnp.maximum(m, jnp.max(u, axis=0, keepdims=True))
    p = jnp.exp2(u - m_new)
    alpha = jnp.exp2(m - m_new)
    l = alpha * l + jnp.sum(p, axis=0, keepdims=True)
    acc = alpha * acc + jnp.dot(v_t, p.astype(MXU_DTYPE), preferred_element_type=F32)
    return m_new, l, acc


def _pad_rows(x, rows):
    return jnp.concatenate([x, jnp.zeros((rows - x.shape[0], x.shape[1]), x.dtype)], axis=0)


def _mla_cache_kernel(q_ref, cckv_ref, ckr_ref, nckv_ref, nkr_ref, sz_ref, wuv_ref, o_ref, *, tq, past):
    cols = MLA_HEADS * tq
    q = q_ref[...].reshape(cols, MLA_QK)
    q_lat, q_rope = q[:, :MLA_KV_RANK], q[:, MLA_KV_RANK:]

    def tile(ckv32, kr32, k0, rows, masked, carry):
        ckv = ckv32.astype(MXU_DTYPE)
        u = (_dot_nt(ckv, q_lat) + _dot_nt(kr32.astype(MXU_DTYPE), q_rope)) * MLA_C1
        if masked:
            u = jnp.where(_visible_t(past, k0, tq, rows, past + tq, cols), u, NEG_INF)
        return _online_softmax_t(u, carry, ckv32.T.astype(MXU_DTYPE))

    def cache_body(j, carry):
        k0 = pl.multiple_of(j * CACHE_TILE, CACHE_TILE)
        return tile(cckv_ref[pl.ds(k0, CACHE_TILE), :], ckr_ref[pl.ds(k0, CACHE_TILE), :], k0, CACHE_TILE,
                    False, carry)

    carry = (jnp.full((1, cols), NEG_INF, F32), jnp.zeros((1, cols), F32), jnp.zeros((MLA_KV_RANK, cols), F32))
    carry = lax.fori_loop(0, past // CACHE_TILE, cache_body, carry)
    _, l, acc = tile(_pad_rows(nckv_ref[...], LANES), _pad_rows(nkr_ref[...], LANES), past, LANES, True, carry)
    o = (acc * (1.0 / l)).T
    o_cat = jnp.concatenate([o[h * tq:(h + 1) * tq] for h in range(MLA_HEADS)], axis=1)
    o_ref[...] = jnp.dot(o_cat.astype(MXU_DTYPE), wuv_ref[...], preferred_element_type=F32) * sz_ref[...]


def _mla_cache_call(q, cache_ckv, cache_kr, ckv, kr, sz, w_uv_bd, batch, seq_len, past):
    assert MLA_HEADS * seq_len == LANES and past % CACHE_TILE == 0 and past > 0
    kern = functools.partial(_mla_cache_kernel, tq=seq_len, past=past)
    return pl.pallas_call(
        kern, grid=(batch,),
        in_specs=[pl.BlockSpec((MLA_HEADS, seq_len, MLA_QK), lambda b: (0, b, 0)),
                  pl.BlockSpec((None, past, MLA_KV_RANK), lambda b: (b, 0, 0)),
                  pl.BlockSpec((None, past, MLA_ROPE), lambda b: (b, 0, 0)),
                  pl.BlockSpec((seq_len, MLA_KV_RANK), lambda b: (b, 0)),
                  pl.BlockSpec((seq_len, MLA_ROPE), lambda b: (b, 0)),
                  pl.BlockSpec((seq_len, MLA_WIDTH), lambda b: (b, 0)),
                  _full((MLA_HEADS * MLA_KV_RANK, MLA_WIDTH))],
        out_specs=pl.BlockSpec((seq_len, MLA_WIDTH), lambda b: (b, 0)),
        out_shape=jax.ShapeDtypeStruct((batch * seq_len, MLA_WIDTH), F32),
        compiler_params=_params(("parallel",)), name="mla_cache",
    )(q, cache_ckv, cache_kr, ckv, kr, sz, w_uv_bd)


def _diff_cache_kernel(lam_init_ref, lamp_ref, slope_ref, q_ref, ck_ref, cv_ref, nk_ref, nv_ref, sz_ref, g_ref,
                       o_ref, *, tq, past):
    n_maps = 4
    pair_w = 2 * DIFF_V
    used = n_maps * tq
    qf = q_ref[...].astype(F32)
    feat_map = lax.broadcasted_iota(jnp.int32, (tq, pair_w), 1) // DIFF_D
    w_t = _pad_rows(jnp.concatenate([jnp.where(feat_map == c, qf, 0.0) for c in range(n_maps)], axis=0),
                    LANES).astype(MXU_DTYPE)
    col = lax.broadcasted_iota(jnp.int32, (1, LANES), 1)
    slope_row = jnp.where(col < used // 2, slope_ref[0:1, 0:1], slope_ref[1:2, 0:1]) * LOG2E
    qpos_row = (past + col % tq).astype(F32)

    def tile(k32, v32, k0, rows, masked, carry):
        s = _dot_nt(k32.astype(MXU_DTYPE), w_t) * DIFF_C1
        kpos = (k0 + lax.broadcasted_iota(jnp.int32, (rows, LANES), 0)).astype(F32)
        u = s - slope_row * jnp.abs(kpos - qpos_row)
        if masked:
            u = jnp.where(_visible_t(past, k0, tq, rows, past + tq, LANES), u, NEG_INF)
        return _online_softmax_t(u, carry, v32.T.astype(MXU_DTYPE))

    def cache_body(j, carry):
        k0 = pl.multiple_of(j * CACHE_TILE, CACHE_TILE)
        return tile(ck_ref[pl.ds(k0, CACHE_TILE), :], cv_ref[pl.ds(k0, CACHE_TILE), :], k0, CACHE_TILE, False,
                    carry)

    carry = (jnp.full((1, LANES), NEG_INF, F32), jnp.zeros((1, LANES), F32), jnp.zeros((pair_w, LANES), F32))
    carry = lax.fori_loop(0, past // CACHE_TILE, cache_body, carry)
    _, l, acc = tile(_pad_rows(nk_ref[...], LANES), _pad_rows(nv_ref[...], LANES), past, LANES, True, carry)

    lam_init = lam_init_ref[0]
    lp = lamp_ref[...]
    lam = (jnp.exp(jnp.sum(lp[0:1] * lp[1:2], axis=-1, keepdims=True))
           - jnp.exp(jnp.sum(lp[2:3] * lp[3:4], axis=-1, keepdims=True)) + lam_init)
    a = (acc * (1.0 / l)).T
    heads = [a[(2 * h) * tq:(2 * h + 1) * tq] - lam * a[(2 * h + 1) * tq:(2 * h + 2) * tq] for h in range(2)]
    lane = lax.broadcasted_iota(jnp.int32, (tq, pair_w), 1)
    first = lane < DIFF_V
    o = jnp.where(first, heads[0], heads[1])
    sq = o * o
    ms0 = jnp.sum(jnp.where(first, sq, 0.0), axis=-1, keepdims=True)
    ms1 = jnp.sum(jnp.where(first, 0.0, sq), axis=-1, keepdims=True)
    ms = jnp.where(first, ms0, ms1) * (1.0 / DIFF_V)
    y = o * lax.rsqrt(ms + EPS) * g_ref[...]
    o_ref[...] = y * (1.0 - lam_init) * sz_ref[...]


def _diff_cache_call(lam_init_arr, lam_params, slopes, qd, cache_k, cache_v, kd, vd, sz, subln2, batch, seq_len,
                     past):
    assert 4 * seq_len <= LANES and seq_len % SUBLANES == 0 and past % CACHE_TILE == 0 and past > 0
    pair_w = 2 * DIFF_V
    kern = functools.partial(_diff_cache_kernel, tq=seq_len, past=past)
    new = pl.BlockSpec((seq_len, pair_w), lambda b, p: (b, p))
    cache = pl.BlockSpec((None, past, pair_w), lambda b, p: (b, 0, p))
    return pl.pallas_call(
        kern, grid=(batch, DIFF_PAIRS),
        in_specs=[pl.BlockSpec(memory_space=pltpu.SMEM), _full((4, DIFF_D)),
                  pl.BlockSpec((None, 2, LANES), lambda b, p: (p, 0, 0)),
                  new, cache, cache, new, new, new, _full((1, pair_w))],
        out_specs=new,
        out_shape=jax.ShapeDtypeStruct((batch * seq_len, DIFF_WIDTH), F32),
        compiler_params=_params(("parallel", "parallel")), name="diff_cache",
    )(lam_init_arr, lam_params, slopes, qd, cache_k, cache_v, kd, vd, sz, subln2)


_EXT_PAD = SUBLANES


def _lru_kernel(x_ref, sz_ref, conv0_ref, h0_ref, cw_ref, cb_ref, wax_ref, ba_ref, bx_ref, lam_ref,
                o_ref, hlast_ref, convnew_ref, ext_ref, a_ref, b_ref, hs_ref, hc_ref, *, tt, past):
    i = pl.program_id(1)
    hist = CONV_W - 1

    @pl.when(i == 0)
    def _():
        hc_ref[...] = h0_ref[...]
        ext_ref[_EXT_PAD - hist:_EXT_PAD, :] = conv0_ref[...]

    ext_ref[_EXT_PAD:_EXT_PAD + tt, :] = x_ref[...]
    cw = cw_ref[...]
    xc = cb_ref[...]
    for k in range(CONV_W):
        xc = xc + ext_ref[_EXT_PAD - hist + k:_EXT_PAD - hist + k + tt, :] * cw[k:k + 1, :]
    gates = jnp.dot(xc.astype(MXU_DTYPE), wax_ref[...], preferred_element_type=F32)
    r = jax.nn.sigmoid(gates[:, :LRU_WIDTH] + ba_ref[...])
    gi = jax.nn.sigmoid(gates[:, LRU_WIDTH:] + bx_ref[...])
    neg_lam = -lam_ref[...]
    softplus = jnp.maximum(neg_lam, 0.0) + jnp.log1p(jnp.exp(-jnp.abs(neg_lam)))
    log_a = -LRU_C * r * softplus
    a = jnp.exp(log_a)
    qpos = past + i * tt + lax.broadcasted_iota(jnp.int32, (tt, LRU_WIDTH), 0)
    mult = jnp.where(qpos == 0, 1.0, jnp.sqrt(1.0 - a * a))
    a_ref[...] = a
    b_ref[...] = mult * gi * xc

    def row(t, h):
        h = a_ref[pl.ds(t, 1), :] * h + b_ref[pl.ds(t, 1), :]
        hs_ref[pl.ds(t, 1), :] = h
        return h

    h = lax.fori_loop(0, tt, row, hc_ref[...], unroll=8)
    hc_ref[...] = h
    o_ref[...] = hs_ref[...] * sz_ref[...]
    hlast_ref[...] = h
    tail = ext_ref[_EXT_PAD + tt - hist:_EXT_PAD + tt, :]
    convnew_ref[...] = tail
    ext_ref[_EXT_PAD - hist:_EXT_PAD, :] = tail


def _lru_call(x_lru, sz, conv0, h0, lw, batch, seq_len, past):
    tt = min(seq_len, 512)
    assert seq_len % tt == 0 and tt >= CONV_W - 1
    nt = seq_len // tt
    hist = CONV_W - 1
    kern = functools.partial(_lru_kernel, tt=tt, past=past)
    tile = pl.BlockSpec((tt, LRU_WIDTH), lambda b, i: (b * nt + i, 0))
    return pl.pallas_call(
        kern, grid=(batch, nt),
        in_specs=[tile, tile,
                  pl.BlockSpec((None, hist, LRU_WIDTH), lambda b, i: (b, 0, 0)),
                  pl.BlockSpec((None, 1, LRU_WIDTH), lambda b, i: (b, 0, 0)),
                  _full((CONV_W, LRU_WIDTH)), _full((1, LRU_WIDTH)),
                  _full((LRU_WIDTH, 2 * LRU_WIDTH)), _full((1, LRU_WIDTH)), _full((1, LRU_WIDTH)),
                  _full((1, LRU_WIDTH))],
        out_specs=[tile,
                   pl.BlockSpec((None, 1, LRU_WIDTH), lambda b, i: (b, 0, 0)),
                   pl.BlockSpec((None, hist, LRU_WIDTH), lambda b, i: (b, 0, 0))],
        out_shape=[jax.ShapeDtypeStruct((batch * seq_len, LRU_WIDTH), F32),
                   jax.ShapeDtypeStruct((batch, 1, LRU_WIDTH), F32),
                   jax.ShapeDtypeStruct((batch, hist, LRU_WIDTH), F32)],
        scratch_shapes=[pltpu.VMEM((_EXT_PAD + tt, LRU_WIDTH), F32), pltpu.VMEM((tt, LRU_WIDTH), F32),
                        pltpu.VMEM((tt, LRU_WIDTH), F32), pltpu.VMEM((tt, LRU_WIDTH), F32),
                        pltpu.VMEM((1, LRU_WIDTH), F32)],
        compiler_params=_params(("parallel", "arbitrary")), name="lru",
    )(x_lru, sz, conv0, h0, lw['conv_w'], lw['conv_b'], lw['w_ax_bd'], lw['b_a'], lw['b_x'], lw['lam'])


def _merge_kernel(x_ref, g_ref, wg_ref, om_ref, od_ref, ol_ref, wom_ref, wod_ref, wol_ref, wout_ref,
                  fg_ref, o_ref, *, final):
    x = x_ref[...]
    xb = _rms(x, g_ref[...]).astype(MXU_DTYPE)
    merged = None
    for b, (br_ref, w_ref) in enumerate(((om_ref, wom_ref), (od_ref, wod_ref), (ol_ref, wol_ref))):
        gate = jax.nn.sigmoid(jnp.dot(xb, wg_ref[:, b * D_MODEL:(b + 1) * D_MODEL],
                                      preferred_element_type=F32))
        term = gate * jnp.dot(br_ref[...].astype(MXU_DTYPE), w_ref[...], preferred_element_type=F32)
        merged = term if merged is None else merged + term
    y = x + jnp.dot(merged.astype(MXU_DTYPE), wout_ref[...], preferred_element_type=F32)
    if final:
        y = _rms(y, fg_ref[...])
    o_ref[...] = y


def _merge_call(x2, o_mla, o_diff, o_lru, lw, final_g, final):
    n = x2.shape[0]
    tm = min(n, 256)
    row = lambda w: pl.BlockSpec((tm, w), lambda i: (i, 0))
    return pl.pallas_call(
        functools.partial(_merge_kernel, final=final), grid=(n // tm,),
        in_specs=[row(D_MODEL), _full((1, D_MODEL)), _full((D_MODEL, N_BRANCH * D_MODEL)),
                  row(MLA_WIDTH), row(DIFF_WIDTH), row(LRU_WIDTH),
                  _full((MLA_WIDTH, D_MODEL)), _full((DIFF_WIDTH, D_MODEL)), _full((LRU_WIDTH, D_MODEL)),
                  _full((D_MODEL, D_MODEL)), _full((1, D_MODEL))],
        out_specs=row(D_MODEL),
        out_shape=jax.ShapeDtypeStruct((n, D_MODEL), F32),
        compiler_params=_params(("parallel",)), name="merge",
    )(x2, lw['norm'], lw['w_gate'], o_mla, o_diff, o_lru, lw['w_o_mla'], lw['w_o_diff'], lw['w_o_lru'],
      lw['w_out'], final_g)


def _block_diag(blocks):
    n, r, c = blocks.shape
    eye = jnp.eye(n, dtype=blocks.dtype)
    return (blocks[:, :, None, :] * eye[:, None, :, None]).reshape(n * r, n * c)


def _swap_halves(w, groups):
    rows, cols = w.shape
    return w.reshape(rows, groups, 2, cols // groups // 2)[:, :, ::-1, :].reshape(rows, cols)


def _layer_weights(l, norm_g, w_in, mla_q_norm, mla_kv_norm, mla_w_uq, mla_w_uk, mla_w_uv, diff_subln,
                   lru_conv_w, lru_conv_b, lru_w_a, lru_b_a, lru_w_x, lru_b_x, lru_lambda, w_o_mla,
                   w_o_diff, w_o_lru, w_out):
    cast = lambda w: w.astype(MXU_DTYPE)
    names = ('c_q', 'c_kv', 'k_r', 'z_mla', 'q_d', 'k_d', 'v_d', 'z_diff', 'x_lru', 'z_lru', 'gate')
    widths = (MLA_Q_RANK, MLA_KV_RANK, MLA_ROPE, MLA_WIDTH, DIFF_WIDTH, DIFF_WIDTH, DIFF_WIDTH, DIFF_WIDTH,
              LRU_WIDTH, LRU_WIDTH, N_BRANCH * D_MODEL)
    cols, off = {}, 0
    for name, w in zip(names, widths):
        cols[name] = w_in[l][:, off:off + w]
        off += w
    uq = mla_w_uq[l].reshape(MLA_Q_RANK, MLA_HEADS, MLA_NOPE + MLA_ROPE)
    uq_rope = uq[:, :, MLA_NOPE:].reshape(MLA_Q_RANK, MLA_HEADS * MLA_ROPE)
    w_kr = cols['k_r']
    return {
        'norm': norm_g[l][None],
        'w_main': cast(jnp.concatenate([cols[n] for n, _ in _MAIN], axis=1)),
        'w_kr': cast(w_kr), 'w_kr_sw': cast(_swap_halves(w_kr, 1)),
        'w_gate': cast(cols['gate']),
        'q_norm': mla_q_norm[l][None], 'kv_norm': mla_kv_norm[l][None],
        'w_uq_nope': cast(uq[:, :, :MLA_NOPE].reshape(MLA_Q_RANK, MLA_HEADS * MLA_NOPE)),
        'w_uq_rope': cast(uq_rope), 'w_uq_rope_sw': cast(_swap_halves(uq_rope, MLA_HEADS)),
        'w_uk_bd': cast(_block_diag(mla_w_uk[l].transpose(1, 2, 0))),
        'w_uv_bd': cast(_block_diag(mla_w_uv[l].transpose(1, 0, 2))),
        'subln2': jnp.tile(diff_subln[l], 2)[None],
        'conv_w': lru_conv_w[l], 'conv_b': lru_conv_b[l][None],
        'w_ax_bd': cast(jnp.concatenate([_block_diag(lru_w_a[l]), _block_diag(lru_w_x[l])], axis=1)),
        'b_a': lru_b_a[l][None], 'b_x': lru_b_x[l][None], 'lam': lru_lambda[l][None],
        'w_o_mla': cast(w_o_mla[l]), 'w_o_diff': cast(w_o_diff[l]), 'w_o_lru': cast(w_o_lru[l]),
        'w_out': cast(w_out[l]),
    }


def _rope_tables(past, seq_len):
    half = MLA_ROPE // 2
    inv = ROPE_BASE ** (-jnp.arange(half, dtype=F32) / half)
    ang = (past + jnp.arange(seq_len, dtype=jnp.int32)).astype(F32)[:, None] * inv[None, :]
    cos, sin = jnp.cos(ang), jnp.sin(ang)
    return (jnp.tile(jnp.concatenate([cos, cos], axis=1), (1, MLA_HEADS)),
            jnp.tile(jnp.concatenate([-sin, sin], axis=1), (1, MLA_HEADS)))


ATTN_TQ = 256
ATTN_TK = 256


def _layer(x2, past_state, lw, layer_idx, lam_params, batch, seq_len, past, final_g, final):
    cache_ckv, cache_kr, cache_dk, cache_dv, h0, conv0 = past_state
    keys_on_rows = past == 0
    tq, tk = ATTN_TQ, ATTN_TK
    assert not keys_on_rows or (seq_len % tq == 0 and seq_len % tk == 0)
    cos_tab, sin_tab = _rope_tables(past, seq_len)
    ops = _proj_call(x2, lw, cos_tab, sin_tab, seq_len, keys_on_rows)

    lam_init = 0.8 - 0.6 * math.exp(-0.3 * layer_idx)
    lam_init_arr = jnp.array([lam_init], F32)
    slope = 2.0 ** (-8.0 * jnp.arange(1, DIFF_HEADS + 1, dtype=F32) / DIFF_HEADS)
    slopes = jnp.broadcast_to(slope.reshape(DIFF_PAIRS, 2, 1), (DIFF_PAIRS, 2, LANES))
    if keys_on_rows:
        o_mla = _mla_kr_call(ops['q'], ops['kcat'], ops['ckv_t'], ops['sz_mla'], lw['w_uv_bd'].T, batch, seq_len,
                             past, seq_len, tq, tk)
        o_diff = _diff_kr_call(lam_init_arr, lam_params, slopes, ops['qd'], ops['kd_rows'], ops['vd_t'],
                               ops['sz_diff'], lw['subln2'], batch, seq_len, past, seq_len, tq, tk)
    else:
        o_mla = _mla_cache_call(ops['q'], cache_ckv, cache_kr, ops['ckv'], ops['kr'], ops['sz_mla'], lw['w_uv_bd'],
                                batch, seq_len, past)
        o_diff = _diff_cache_call(lam_init_arr, lam_params, slopes, ops['qd'],
                                  cache_dk.reshape(batch, past, DIFF_WIDTH),
                                  cache_dv.reshape(batch, past, DIFF_WIDTH), ops['kd'], ops['vd'], ops['sz_diff'],
                                  lw['subln2'], batch, seq_len, past)
    ckv, kr, kd, vd = ops['ckv'], ops['kr'], ops['kd'], ops['vd']

    o_lru, h_last, conv_new = _lru_call(ops['x_lru'], ops['sz_lru'], conv0, h0[:, None, :], lw, batch, seq_len,
                                        past)

    x_new = _merge_call(x2, o_mla, o_diff, o_lru, lw, final_g, final)
    states = (ckv.reshape(batch, seq_len, MLA_KV_RANK), kr.reshape(batch, seq_len, MLA_ROPE),
              kd.reshape(batch, seq_len, DIFF_HEADS, DIFF_V), vd.reshape(batch, seq_len, DIFF_HEADS, DIFF_V),
              h_last[:, 0, :], conv_new)
    return x_new, states


def kernel(x_prompt, x_sample, cache_mla_ckv, cache_mla_krope, cache_diff_k, cache_diff_v, state_lru_h, state_lru_conv, norm_g, w_in, mla_q_norm, mla_kv_norm, mla_w_uq, mla_w_uk, mla_w_uv, diff_lq1, diff_lk1, diff_lq2, diff_lk2, diff_subln, lru_conv_w, lru_conv_b, lru_w_a, lru_b_a, lru_w_x, lru_b_x, lru_lambda, w_o_mla, w_o_diff, w_o_lru, w_out, final_norm):
    depth = w_in.shape[0]
    bp, tp, _ = x_prompt.shape
    bs, ts, _ = x_sample.shape
    past = cache_mla_ckv.shape[2]
    zeros_h = jnp.zeros((bp, LRU_WIDTH), F32)
    zeros_conv = jnp.zeros((bp, CONV_W - 1, LRU_WIDTH), F32)
    xp = x_prompt.reshape(bp * tp, D_MODEL)
    xs = x_sample.reshape(bs * ts, D_MODEL)
    final_g = final_norm[None]
    p_states, s_states = [], []
    for l in range(depth):
        lw = _layer_weights(l, norm_g, w_in, mla_q_norm, mla_kv_norm, mla_w_uq, mla_w_uk, mla_w_uv,
                            diff_subln, lru_conv_w, lru_conv_b, lru_w_a, lru_b_a, lru_w_x, lru_b_x,
                            lru_lambda, w_o_mla, w_o_diff, w_o_lru, w_out)
        lam_params = jnp.stack([diff_lq1[l], diff_lk1[l], diff_lq2[l], diff_lk2[l]])
        final = l == depth - 1
        xp, st_p = _layer(xp, (None, None, None, None, zeros_h, zeros_conv), lw, l, lam_params, bp, tp, 0,
                          final_g, final)
        xs, st_s = _layer(xs, (cache_mla_ckv[l], cache_mla_krope[l], cache_diff_k[l], cache_diff_v[l],
                               state_lru_h[l], state_lru_conv[l]), lw, l, lam_params, bs, ts, past,
                          final_g, final)
        p_states.append(st_p)
        s_states.append(st_s)
    stack = lambda states, i: jnp.stack([s[i] for s in states], axis=0)
    return (xp.reshape(bp, tp, D_MODEL), xs.reshape(bs, ts, D_MODEL),
            *(stack(p_states, i) for i in range(6)), *(stack(s_states, i) for i in range(6)))
```

```python
import functools
import math

import jax
import jax.numpy as jnp
from jax import lax
from jax.experimental import pallas as pl
from jax.experimental.pallas import tpu as pltpu

F32 = jnp.float32
MXU_DTYPE = jnp.bfloat16

D_MODEL = 1024
CHUNK = 64
CHUNK_SHIFT = 6
EPS = 1e-6
NEG_INF = -1e30
LOG2E = math.log2(math.e)

MLA_HEADS = 8
MLA_Q_RANK = 256
MLA_KV_RANK = 128
MLA_NOPE = 64
MLA_ROPE = 32
MLA_V = 64
MLA_WIDTH = MLA_HEADS * MLA_V
MLA_QK = MLA_KV_RANK + MLA_ROPE
ROPE_BASE = 10000.0

DIFF_HEADS = 8
DIFF_D = 32
DIFF_V = 2 * DIFF_D
DIFF_WIDTH = DIFF_HEADS * DIFF_V
DIFF_PAIRS = DIFF_HEADS // 2
MLA_C1 = (MLA_NOPE + MLA_ROPE) ** -0.5 * LOG2E
DIFF_C1 = DIFF_D ** -0.5 * LOG2E

LRU_WIDTH = 512
LRU_BLOCKS = 8
CONV_W = 4
LRU_C = 8.0
N_BRANCH = 3

V7X_VMEM_BYTES = 64 * 1024 * 1024
LANES = 128
SUBLANES = 8
PACKED_ROWS = 16
V_ROWS = 2 * DIFF_V + PACKED_ROWS
VMEM_LIMIT = V7X_VMEM_BYTES * 3 // 4

_MAIN = (('c_q', MLA_Q_RANK), ('c_kv', MLA_KV_RANK), ('z_mla', MLA_WIDTH), ('q_d', DIFF_WIDTH),
         ('k_d', DIFF_WIDTH), ('v_d', DIFF_WIDTH), ('z_diff', DIFF_WIDTH), ('x_lru', LRU_WIDTH),
         ('z_lru', LRU_WIDTH))
_MAIN_OFF = {}
_o = 0
for _n, _w in _MAIN:
    _MAIN_OFF[_n] = (_o, _o + _w)
    _o += _w
MAIN_WIDTH = _o


def _params(sem):
    return pltpu.CompilerParams(dimension_semantics=sem, vmem_limit_bytes=VMEM_LIMIT)


def _rms(x, g):
    return x * lax.rsqrt(jnp.mean(x * x, axis=-1, keepdims=True) + EPS) * g


def _silu(z):
    return z * jax.nn.sigmoid(z)


def _full(shape):
    return pl.BlockSpec(shape, lambda *_: (0,) * len(shape))


_PROJ_STATE_OUTS = ('ckv', 'kr', 'sz_mla', 'kd', 'vd', 'sz_diff', 'x_lru', 'sz_lru')
_PROJ_KR_OUTS = ('q', 'qd', 'kcat', 'ckv_t', 'kd_rows', 'vd_t') + _PROJ_STATE_OUTS
_PROJ_CACHE_OUTS = ('q', 'qd') + _PROJ_STATE_OUTS


def _proj_kernel(x_ref, g_ref, wm_ref, wkr_ref, wkrs_ref, qg_ref, kvg_ref, wuqn_ref, wuqr_ref,
                 wuqrs_ref, wuk_ref, cos_ref, sin_ref, *out_refs, keys_on_rows):
    o = dict(zip(_PROJ_KR_OUTS if keys_on_rows else _PROJ_CACHE_OUTS, out_refs))
    q_ref, qd_ref = o['q'], o['qd']
    xb = _rms(x_ref[...], g_ref[...]).astype(MXU_DTYPE)

    def proj(name):
        a, b = _MAIN_OFF[name]
        return jnp.dot(xb, wm_ref[:, a:b], preferred_element_type=F32)

    cos = cos_ref[...]
    sin = sin_ref[...]

    cq = _rms(proj('c_q'), qg_ref[...]).astype(MXU_DTYPE)
    q_nope = jnp.dot(cq, wuqn_ref[...], preferred_element_type=F32)
    q_rot = (jnp.dot(cq, wuqr_ref[...], preferred_element_type=F32) * cos
             + jnp.dot(cq, wuqrs_ref[...], preferred_element_type=F32) * sin)
    q_lat = jnp.dot(q_nope.astype(MXU_DTYPE), wuk_ref[...], preferred_element_type=F32)
    if keys_on_rows:
        q_lat_t = (q_lat * MLA_C1).T.astype(q_ref.dtype)
        q_rot_t = (q_rot * MLA_C1).T.astype(q_ref.dtype)
        for h in range(MLA_HEADS):
            q_ref[h, 0:MLA_KV_RANK, :] = q_lat_t[h * MLA_KV_RANK:(h + 1) * MLA_KV_RANK]
            q_ref[h, MLA_KV_RANK:MLA_QK, :] = q_rot_t[h * MLA_ROPE:(h + 1) * MLA_ROPE]
    else:
        for h in range(MLA_HEADS):
            q_ref[h, :, 0:MLA_KV_RANK] = q_lat[:, h * MLA_KV_RANK:(h + 1) * MLA_KV_RANK].astype(q_ref.dtype)
            q_ref[h, :, MLA_KV_RANK:MLA_QK] = q_rot[:, h * MLA_ROPE:(h + 1) * MLA_ROPE].astype(q_ref.dtype)

    ckv = _rms(proj('c_kv'), kvg_ref[...])
    kr = (jnp.dot(xb, wkr_ref[...], preferred_element_type=F32) * cos[:, :MLA_ROPE]
          + jnp.dot(xb, wkrs_ref[...], preferred_element_type=F32) * sin[:, :MLA_ROPE])
    o['ckv'][...] = ckv
    o['kr'][...] = kr
    if keys_on_rows:
        ones_rows = jnp.where(lax.broadcasted_iota(jnp.int32, (PACKED_ROWS, x_ref.shape[0]), 0) == 0,
                              1.0, 0.0).astype(MXU_DTYPE)
        o['kcat'][:, 0:MLA_KV_RANK] = ckv.astype(MXU_DTYPE)
        o['kcat'][:, MLA_KV_RANK:MLA_QK] = kr.astype(MXU_DTYPE)
        o['ckv_t'][0:MLA_KV_RANK, :] = ckv.T.astype(MXU_DTYPE)
        o['ckv_t'][MLA_KV_RANK:V_ROWS, :] = ones_rows
    o['sz_mla'][...] = _silu(proj('z_mla'))

    qd = proj('q_d')
    kd = proj('k_d')
    vd = proj('v_d')
    o['kd'][...] = kd
    o['vd'][...] = vd
    if keys_on_rows:
        qd_ref[...] = (qd * DIFF_C1).T.astype(qd_ref.dtype)
        o['kd_rows'][...] = kd.astype(MXU_DTYPE)
        vd_t = vd.T.astype(MXU_DTYPE)
        pair_w = 2 * DIFF_V
        for p in range(DIFF_PAIRS):
            o['vd_t'][p * V_ROWS:p * V_ROWS + pair_w, :] = vd_t[p * pair_w:(p + 1) * pair_w]
            o['vd_t'][p * V_ROWS + pair_w:(p + 1) * V_ROWS, :] = ones_rows
    else:
        qd_ref[...] = qd.astype(qd_ref.dtype)
    o['sz_diff'][...] = _silu(proj('z_diff'))

    o['x_lru'][...] = proj('x_lru')
    o['sz_lru'][...] = _silu(proj('z_lru'))


def _proj_call(x2, lw, cos_tab, sin_tab, seq_len, keys_on_rows):
    n = x2.shape[0]
    tm = min(n, 256)
    assert n % tm == 0
    if seq_len % tm == 0:
        per = seq_len // tm
        tab_map = lambda i: (i % per, 0)
    else:
        assert tm % seq_len == 0
        reps = tm // seq_len
        cos_tab = jnp.tile(cos_tab, (reps, 1))
        sin_tab = jnp.tile(sin_tab, (reps, 1))
        tab_map = lambda i: (0, 0)
    row = lambda w: pl.BlockSpec((tm, w), lambda i: (i, 0))
    col = lambda w: pl.BlockSpec((w, tm), lambda i: (0, i))
    rope_w = MLA_HEADS * MLA_ROPE
    in_specs = [row(D_MODEL), _full((1, D_MODEL)), _full((D_MODEL, MAIN_WIDTH)),
                _full((D_MODEL, MLA_ROPE)), _full((D_MODEL, MLA_ROPE)),
                _full((1, MLA_Q_RANK)), _full((1, MLA_KV_RANK)),
                _full((MLA_Q_RANK, MLA_HEADS * MLA_NOPE)), _full((MLA_Q_RANK, rope_w)),
                _full((MLA_Q_RANK, rope_w)), _full((MLA_HEADS * MLA_NOPE, MLA_HEADS * MLA_KV_RANK)),
                pl.BlockSpec((tm, rope_w), tab_map), pl.BlockSpec((tm, rope_w), tab_map)]
    rows_f32 = lambda w: (jax.ShapeDtypeStruct((n, w), F32), row(w))
    rows_mxu = lambda w: (jax.ShapeDtypeStruct((n, w), MXU_DTYPE), row(w))
    cols_mxu = lambda w: (jax.ShapeDtypeStruct((w, n), MXU_DTYPE), col(w))
    outs = {'ckv': rows_f32(MLA_KV_RANK), 'kr': rows_f32(MLA_ROPE), 'sz_mla': rows_f32(MLA_WIDTH),
            'kd': rows_f32(DIFF_WIDTH), 'vd': rows_f32(DIFF_WIDTH), 'sz_diff': rows_f32(DIFF_WIDTH),
            'x_lru': rows_f32(LRU_WIDTH), 'sz_lru': rows_f32(LRU_WIDTH)}
    if keys_on_rows:
        outs.update({'q': (jax.ShapeDtypeStruct((MLA_HEADS, MLA_QK, n), MXU_DTYPE),
                           pl.BlockSpec((MLA_HEADS, MLA_QK, tm), lambda i: (0, 0, i))),
                     'qd': cols_mxu(DIFF_WIDTH), 'kcat': rows_mxu(MLA_QK), 'ckv_t': cols_mxu(V_ROWS),
                     'kd_rows': rows_mxu(DIFF_WIDTH), 'vd_t': cols_mxu(DIFF_PAIRS * V_ROWS)})
    else:
        outs.update({'q': (jax.ShapeDtypeStruct((MLA_HEADS, n, MLA_QK), MXU_DTYPE),
                           pl.BlockSpec((MLA_HEADS, tm, MLA_QK), lambda i: (0, i, 0))),
                     'qd': rows_mxu(DIFF_WIDTH)})
    names = _PROJ_KR_OUTS if keys_on_rows else _PROJ_CACHE_OUTS
    arrays = pl.pallas_call(
        functools.partial(_proj_kernel, keys_on_rows=keys_on_rows), grid=(n // tm,), in_specs=in_specs,
        out_specs=[outs[k][1] for k in names], out_shape=[outs[k][0] for k in names],
        compiler_params=_params(("parallel",)), name="proj",
    )(x2, lw['norm'], lw['w_main'], lw['w_kr'], lw['w_kr_sw'], lw['q_norm'], lw['kv_norm'],
      lw['w_uq_nope'], lw['w_uq_rope'], lw['w_uq_rope_sw'], lw['w_uk_bd'], cos_tab, sin_tab)
    return dict(zip(names, arrays))


def _tile_counts(q0, tq, tk, total_keys):
    seen_by_all = jnp.minimum(((q0 >> CHUNK_SHIFT) + 1) * CHUNK, total_keys)
    seen_by_any = jnp.minimum((((q0 + tq - 1) >> CHUNK_SHIFT) + 1) * CHUNK, total_keys)
    return seen_by_all // tk, (seen_by_any + tk - 1) // tk


def _visible_t(q0, k0, tq, tk, total_keys, width):
    qpos = q0 + lax.broadcasted_iota(jnp.int32, (tk, width), 1) % tq
    kpos = k0 + lax.broadcasted_iota(jnp.int32, (tk, width), 0)
    never = jnp.int32(jnp.iinfo(jnp.int32).max)
    return jnp.where(kpos < total_keys, kpos >> CHUNK_SHIFT, never) <= (qpos >> CHUNK_SHIFT)


def _pipeline_scratch(tk, cols):
    return [pltpu.VMEM((tk, cols), F32), pltpu.VMEM((tk, cols), F32),
            pltpu.VMEM((tk, cols), MXU_DTYPE), pltpu.VMEM((tk, cols), MXU_DTYPE),
            pltpu.VMEM((1, cols), F32), pltpu.VMEM((1, cols), F32)]


def _pipelined_tiles(n, n_blocks, prepare, scores, softmax, accumulate, s_refs, p_refs, a_refs):
    last = jnp.maximum(n - 1, 0)
    width = s_refs[0].shape[1] // n_blocks
    blocks = [(c, slice(c * width, (c + 1) * width)) for c in range(n_blocks)]
    p_refs[1][...] = jnp.zeros(p_refs[1].shape, p_refs[1].dtype)
    a_refs[1][...] = jnp.ones(a_refs[1].shape, F32)
    for c, cs in blocks:
        s_refs[0][:, cs] = scores(0, c)

    def half(a, cur, nxt):
        ctx = prepare(a)
        for c, cs in blocks:
            accumulate(jnp.maximum(a - 1, 0), c, p_refs[nxt][:, cs], a_refs[nxt][:, cs])
            s_refs[nxt][:, cs] = scores(jnp.minimum(a + 1, last), c)
            p_refs[cur][:, cs], a_refs[cur][:, cs] = softmax(ctx, c, s_refs[cur][:, cs])

    def body(t, carry):
        a = 2 * t
        half(a, 0, 1)

        @pl.when(a + 1 < n)
        def _():
            half(a + 1, 1, 0)
        return carry

    lax.fori_loop(0, (n + 1) // 2, body, 0)
    odd = n % 2 == 1

    @pl.when(odd)
    def _():
        for c, cs in blocks:
            accumulate(last, c, p_refs[0][:, cs], a_refs[0][:, cs])

    @pl.when(jnp.logical_not(odd))
    def _():
        for c, cs in blocks:
            accumulate(last, c, p_refs[1][:, cs], a_refs[1][:, cs])


def _unpipelined_tiles(lo, hi, n_blocks, prepare, scores, softmax, accumulate):
    def body(j, carry):
        ctx = prepare(j)
        s = [scores(j, c) for c in range(n_blocks)]
        pa = [softmax(ctx, c, s[c]) for c in range(n_blocks)]
        for c in range(n_blocks):
            accumulate(j, c, *pa[c])
        return carry

    lax.fori_loop(lo, hi, body, 0)


def _mla_kr_kernel(q_ref, k_ref, vt_ref, sz_ref, wuvt_ref, o_ref, m_ref, acc_ref, s0_ref, s1_ref,
                   p0_ref, p1_ref, a0_ref, a1_ref, *, tq, tk, past, total_keys):
    i = pl.program_id(1)
    s_refs, p_refs, a_refs = (s0_ref, s1_ref), (p0_ref, p1_ref), (a0_ref, a1_ref)
    cols = MLA_HEADS * tq
    m_ref[...] = jnp.full(m_ref.shape, NEG_INF, F32)
    acc_ref[...] = jnp.zeros(acc_ref.shape, F32)
    q0 = past + i * tq
    n_full, n_any = _tile_counts(q0, tq, tk, total_keys)

    def scores(j, h):
        k0 = pl.multiple_of(j * tk, tk)
        return jnp.dot(k_ref[pl.ds(k0, tk), :], q_ref[h], preferred_element_type=F32)

    def accumulate(j, h, p, alpha):
        k0 = pl.multiple_of(j * tk, tk)
        cs = slice(h * tq, (h + 1) * tq)
        pv = jnp.dot(vt_ref[:, pl.ds(k0, tk)], p, preferred_element_type=F32)
        acc_ref[:, cs] = alpha * acc_ref[:, cs] + pv

    def softmax(vis, h, s):
        cs = slice(h * tq, (h + 1) * tq)
        if vis is not None:
            s = jnp.where(vis, s, NEG_INF)
        m_prev = m_ref[:, cs]
        m_new = jnp.maximum(m_prev, jnp.max(s, axis=0, keepdims=True))
        m_ref[:, cs] = m_new
        return jnp.exp2(s - m_new).astype(MXU_DTYPE), jnp.exp2(m_prev - m_new)

    _pipelined_tiles(n_full, MLA_HEADS, lambda j: None, scores, softmax, accumulate, s_refs, p_refs, a_refs)
    _unpipelined_tiles(n_full, n_any, MLA_HEADS, lambda j: _visible_t(q0, j * tk, tq, tk, total_keys, tq),
                       scores, softmax, accumulate)

    o_t = acc_ref[0:MLA_KV_RANK, :] * (1.0 / acc_ref[MLA_KV_RANK:MLA_KV_RANK + 1, :])
    o_cat_t = jnp.concatenate([o_t[:, h * tq:(h + 1) * tq] for h in range(MLA_HEADS)], axis=0)
    out_t = jnp.dot(wuvt_ref[...], o_cat_t.astype(MXU_DTYPE), preferred_element_type=F32)
    o_ref[...] = out_t.T * sz_ref[...]


def _mla_kr_call(q_t, kcat_all, ckvt_all, sz, w_uv_bd_t, batch, seq_len, past, keys_padded, tq, tk):
    nq = seq_len // tq
    cols = MLA_HEADS * tq
    kern = functools.partial(_mla_kr_kernel, tq=tq, tk=tk, past=past, total_keys=past + seq_len)
    return pl.pallas_call(
        kern, grid=(batch, nq),
        in_specs=[pl.BlockSpec((MLA_HEADS, MLA_QK, tq), lambda b, i: (0, 0, b * nq + i)),
                  pl.BlockSpec((keys_padded, MLA_QK), lambda b, i: (b, 0)),
                  pl.BlockSpec((V_ROWS, keys_padded), lambda b, i: (0, b)),
                  pl.BlockSpec((tq, MLA_WIDTH), lambda b, i: (b * nq + i, 0)),
                  _full((MLA_WIDTH, MLA_HEADS * MLA_KV_RANK))],
        out_specs=pl.BlockSpec((tq, MLA_WIDTH), lambda b, i: (b * nq + i, 0)),
        out_shape=jax.ShapeDtypeStruct((batch * seq_len, MLA_WIDTH), F32),
        scratch_shapes=[pltpu.VMEM((1, cols), F32), pltpu.VMEM((V_ROWS, cols), F32)]
        + _pipeline_scratch(tk, cols),
        compiler_params=_params(("parallel", "parallel")), name="mla_kr",
    )(q_t, kcat_all, ckvt_all, sz, w_uv_bd_t)


def _diff_kr_kernel(lam_init_ref, lamp_ref, slope_ref, qt_ref, k_ref, vt_ref, sz_ref, g_ref, o_ref,
                    m_ref, acc_ref, s0_ref, s1_ref, p0_ref, p1_ref, a0_ref, a1_ref, *, tq, tk, past,
                    total_keys):
    i = pl.program_id(2)
    s_refs, p_refs, a_refs = (s0_ref, s1_ref), (p0_ref, p1_ref), (a0_ref, a1_ref)
    n_maps = 4
    pair_w = 2 * DIFF_V
    qt = qt_ref[...].astype(F32)
    feat_map = lax.broadcasted_iota(jnp.int32, (pair_w, tq), 0) // DIFF_D
    w_all = jnp.concatenate([jnp.where(feat_map == c, qt, 0.0) for c in range(n_maps)],
                            axis=1).astype(MXU_DTYPE)
    slope2 = [slope_ref[0:1, 0:1] * LOG2E, slope_ref[1:2, 0:1] * LOG2E]
    m_ref[...] = jnp.full(m_ref.shape, NEG_INF, F32)
    acc_ref[...] = jnp.zeros(acc_ref.shape, F32)
    q0 = past + i * tq
    n_full, n_any = _tile_counts(q0, tq, tk, total_keys)
    rel = (lax.broadcasted_iota(jnp.int32, (tk, tq), 0)
           - lax.broadcasted_iota(jnp.int32, (tk, tq), 1)).astype(F32)

    def scores(j, c):
        k0 = pl.multiple_of(j * tk, tk)
        return jnp.dot(k_ref[pl.ds(k0, tk), :], w_all[:, c * tq:(c + 1) * tq], preferred_element_type=F32)

    def accumulate(j, c, p, alpha):
        k0 = pl.multiple_of(j * tk, tk)
        cs = slice(c * tq, (c + 1) * tq)
        pv = jnp.dot(vt_ref[:, pl.ds(k0, tk)], p, preferred_element_type=F32)
        acc_ref[:, cs] = alpha * acc_ref[:, cs] + pv

    def prepare(j, masked):
        vis = _visible_t(q0, j * tk, tq, tk, total_keys, tq) if masked else None
        return (j * tk - q0).astype(F32), vis

    def softmax(ctx, c, s):
        offset, vis = ctx
        cs = slice(c * tq, (c + 1) * tq)
        u = s - slope2[c // 2] * jnp.abs(rel + offset)
        if vis is not None:
            u = jnp.where(vis, u, NEG_INF)
        m_prev = m_ref[:, cs]
        m_new = jnp.maximum(m_prev, jnp.max(u, axis=0, keepdims=True))
        m_ref[:, cs] = m_new
        return jnp.exp2(u - m_new).astype(MXU_DTYPE), jnp.exp2(m_prev - m_new)

    _pipelined_tiles(n_full, n_maps, lambda j: prepare(j, False), scores, softmax, accumulate, s_refs, p_refs,
                     a_refs)
    _unpipelined_tiles(n_full, n_any, n_maps, lambda j: prepare(j, True), scores, softmax, accumulate)

    lam_init = lam_init_ref[0]
    lp = lamp_ref[...]
    lam = (jnp.exp(jnp.sum(lp[0:1] * lp[1:2], axis=-1, keepdims=True))
           - jnp.exp(jnp.sum(lp[2:3] * lp[3:4], axis=-1, keepdims=True)) + lam_init)
    a = acc_ref[0:pair_w, :] * (1.0 / acc_ref[pair_w:pair_w + 1, :])
    halves = []
    for h in range(2):
        rows = slice(h * DIFF_V, (h + 1) * DIFF_V)
        halves.append(a[rows, (2 * h) * tq:(2 * h + 1) * tq] - lam * a[rows, (2 * h + 1) * tq:(2 * h + 2) * tq])
    o = jnp.concatenate(halves, axis=0).T
    lane = lax.broadcasted_iota(jnp.int32, (tq, pair_w), 1)
    first = lane < DIFF_V
    sq = o * o
    ms0 = jnp.sum(jnp.where(first, sq, 0.0), axis=-1, keepdims=True)
    ms1 = jnp.sum(jnp.where(first, 0.0, sq), axis=-1, keepdims=True)
    ms = jnp.where(first, ms0, ms1) * (1.0 / DIFF_V)
    y = o * lax.rsqrt(ms + EPS) * g_ref[...]
    o_ref[...] = y * (1.0 - lam_init) * sz_ref[...]


def _diff_kr_call(lam_init_arr, lam_params, slopes, qd_t, k_all, vt_all, sz, subln2, batch, seq_len, past,
                  keys_padded, tq, tk):
    nq = seq_len // tq
    pair_w = 2 * DIFF_V
    kern = functools.partial(_diff_kr_kernel, tq=tq, tk=tk, past=past, total_keys=past + seq_len)
    return pl.pallas_call(
        kern, grid=(batch, DIFF_PAIRS, nq),
        in_specs=[pl.BlockSpec(memory_space=pltpu.SMEM),
                  _full((4, DIFF_D)),
                  pl.BlockSpec((None, 2, LANES), lambda b, p, i: (p, 0, 0)),
                  pl.BlockSpec((pair_w, tq), lambda b, p, i: (p, b * nq + i)),
                  pl.BlockSpec((keys_padded, pair_w), lambda b, p, i: (b, p)),
                  pl.BlockSpec((V_ROWS, keys_padded), lambda b, p, i: (p, b)),
                  pl.BlockSpec((tq, pair_w), lambda b, p, i: (b * nq + i, p)),
                  _full((1, pair_w))],
        out_specs=pl.BlockSpec((tq, pair_w), lambda b, p, i: (b * nq + i, p)),
        out_shape=jax.ShapeDtypeStruct((batch * seq_len, DIFF_WIDTH), F32),
        scratch_shapes=[pltpu.VMEM((1, 4 * tq), F32), pltpu.VMEM((V_ROWS, 4 * tq), F32)]
        + _pipeline_scratch(tk, 4 * tq),
        compiler_params=_params(("parallel", "parallel", "parallel")), name="diff_kr",
    )(lam_init_arr, lam_params, slopes, qd_t, k_all, vt_all, sz, subln2)


CACHE_TILE = 512


def _dot_nt(a, b):
    return lax.dot_general(a, b, (((1,), (1,)), ((), ())), preferred_element_type=F32)


def _online_softmax_t(u, carry, v_t):
    m, l, acc = carry
    m_new = jnp.maximum(m, jnp.max(u, axis=0, keepdims=True))
    p = jnp.exp2(u - m_new)
    alpha = jnp.exp2(m - m_new)
    l = alpha * l + jnp.sum(p, axis=0, keepdims=True)
    acc = alpha * acc + jnp.dot(v_t, p.astype(MXU_DTYPE), preferred_element_type=F32)
    return m_new, l, acc


def _pad_rows(x, rows):
    return jnp.concatenate([x, jnp.zeros((rows - x.shape[0], x.shape[1]), x.dtype)], axis=0)


def _mla_cache_kernel(q_ref, cckv_ref, ckr_ref, nckv_ref, nkr_ref, sz_ref, wuv_ref, o_ref, *, tq, past):
    cols = MLA_HEADS * tq
    q = q_ref[...].reshape(cols, MLA_QK)
    q_lat, q_rope = q[:, :MLA_KV_RANK], q[:, MLA_KV_RANK:]

    def tile(ckv32, kr32, k0, rows, masked, carry):
        ckv = ckv32.astype(MXU_DTYPE)
        u = (_dot_nt(ckv, q_lat) + _dot_nt(kr32.astype(MXU_DTYPE), q_rope)) * MLA_C1
        if masked:
            u = jnp.where(_visible_t(past, k0, tq, rows, past + tq, cols), u, NEG_INF)
        return _online_softmax_t(u, carry, ckv32.T.astype(MXU_DTYPE))

    def cache_body(j, carry):
        k0 = pl.multiple_of(j * CACHE_TILE, CACHE_TILE)
        return tile(cckv_ref[pl.ds(k0, CACHE_TILE), :], ckr_ref[pl.ds(k0, CACHE_TILE), :], k0, CACHE_TILE,
                    False, carry)

    carry = (jnp.full((1, cols), NEG_INF, F32), jnp.zeros((1, cols), F32), jnp.zeros((MLA_KV_RANK, cols), F32))
    carry = lax.fori_loop(0, past // CACHE_TILE, cache_body, carry)
    _, l, acc = tile(_pad_rows(nckv_ref[...], LANES), _pad_rows(nkr_ref[...], LANES), past, LANES, True, carry)
    o = (acc * (1.0 / l)).T
    o_cat = jnp.concatenate([o[h * tq:(h + 1) * tq] for h in range(MLA_HEADS)], axis=1)
    o_ref[...] = jnp.dot(o_cat.astype(MXU_DTYPE), wuv_ref[...], preferred_element_type=F32) * sz_ref[...]


def _mla_cache_call(q, cache_ckv, cache_kr, ckv, kr, sz, w_uv_bd, batch, seq_len, past):
    assert MLA_HEADS * seq_len == LANES and past % CACHE_TILE == 0 and past > 0
    kern = functools.partial(_mla_cache_kernel, tq=seq_len, past=past)
    return pl.pallas_call(
        kern, grid=(batch,),
        in_specs=[pl.BlockSpec((MLA_HEADS, seq_len, MLA_QK), lambda b: (0, b, 0)),
                  pl.BlockSpec((None, past, MLA_KV_RANK), lambda b: (b, 0, 0)),
                  pl.BlockSpec((None, past, MLA_ROPE), lambda b: (b, 0, 0)),
                  pl.BlockSpec((seq_len, MLA_KV_RANK), lambda b: (b, 0)),
                  pl.BlockSpec((seq_len, MLA_ROPE), lambda b: (b, 0)),
                  pl.BlockSpec((seq_len, MLA_WIDTH), lambda b: (b, 0)),
                  _full((MLA_HEADS * MLA_KV_RANK, MLA_WIDTH))],
        out_specs=pl.BlockSpec((seq_len, MLA_WIDTH), lambda b: (b, 0)),
        out_shape=jax.ShapeDtypeStruct((batch * seq_len, MLA_WIDTH), F32),
        compiler_params=_params(("parallel",)), name="mla_cache",
    )(q, cache_ckv, cache_kr, ckv, kr, sz, w_uv_bd)


def _diff_cache_kernel(lam_init_ref, lamp_ref, slope_ref, q_ref, ck_ref, cv_ref, nk_ref, nv_ref, sz_ref, g_ref,
                       o_ref, *, tq, past):
    n_maps = 4
    pair_w = 2 * DIFF_V
    used = n_maps * tq
    qf = q_ref[...].astype(F32)
    feat_map = lax.broadcasted_iota(jnp.int32, (tq, pair_w), 1) // DIFF_D
    w_t = _pad_rows(jnp.concatenate([jnp.where(feat_map == c, qf, 0.0) for c in range(n_maps)], axis=0),
                    LANES).astype(MXU_DTYPE)
    col = lax.broadcasted_iota(jnp.int32, (1, LANES), 1)
    slope_row = jnp.where(col < used // 2, slope_ref[0:1, 0:1], slope_ref[1:2, 0:1]) * LOG2E
    qpos_row = (past + col % tq).astype(F32)

    def tile(k32, v32, k0, rows, masked, carry):
        s = _dot_nt(k32.astype(MXU_DTYPE), w_t) * DIFF_C1
        kpos = (k0 + lax.broadcasted_iota(jnp.int32, (rows, LANES), 0)).astype(F32)
        u = s - slope_row * jnp.abs(kpos - qpos_row)
        if masked:
            u = jnp.where(_visible_t(past, k0, tq, rows, past + tq, LANES), u, NEG_INF)
        return _online_softmax_t(u, carry, v32.T.astype(MXU_DTYPE))

    def cache_body(j, carry):
        k0 = pl.multiple_of(j * CACHE_TILE, CACHE_TILE)
        return tile(ck_ref[pl.ds(k0, CACHE_TILE), :], cv_ref[pl.ds(k0, CACHE_TILE), :], k0, CACHE_TILE, False,
                    carry)

    carry = (jnp.full((1, LANES), NEG_INF, F32), jnp.zeros((1, LANES), F32), jnp.zeros((pair_w, LANES), F32))
    carry = lax.fori_loop(0, past // CACHE_TILE, cache_body, carry)
    _, l, acc = tile(_pad_rows(nk_ref[...], LANES), _pad_rows(nv_ref[...], LANES), past, LANES, True, carry)

    lam_init = lam_init_ref[0]
    lp = lamp_ref[...]
    lam = (jnp.exp(jnp.sum(lp[0:1] * lp[1:2], axis=-1, keepdims=True))
           - jnp.exp(jnp.sum(lp[2:3] * lp[3:4], axis=-1, keepdims=True)) + lam_init)
    a = (acc * (1.0 / l)).T
    heads = [a[(2 * h) * tq:(2 * h + 1) * tq] - lam * a[(2 * h + 1) * tq:(2 * h + 2) * tq] for h in range(2)]
    lane = lax.broadcasted_iota(jnp.int32, (tq, pair_w), 1)
    first = lane < DIFF_V
    o = jnp.where(first, heads[0], heads[1])
    sq = o * o
    ms0 = jnp.sum(jnp.where(first, sq, 0.0), axis=-1, keepdims=True)
    ms1 = jnp.sum(jnp.where(first, 0.0, sq), axis=-1, keepdims=True)
    ms = jnp.where(first, ms0, ms1) * (1.0 / DIFF_V)
    y = o * lax.rsqrt(ms + EPS) * g_ref[...]
    o_ref[...] = y * (1.0 - lam_init) * sz_ref[...]


def _diff_cache_call(lam_init_arr, lam_params, slopes, qd, cache_k, cache_v, kd, vd, sz, subln2, batch, seq_len,
                     past):
    assert 4 * seq_len <= LANES and seq_len % SUBLANES == 0 and past % CACHE_TILE == 0 and past > 0
    pair_w = 2 * DIFF_V
    kern = functools.partial(_diff_cache_kernel, tq=seq_len, past=past)
    new = pl.BlockSpec((seq_len, pair_w), lambda b, p: (b, p))
    cache = pl.BlockSpec((None, past, pair_w), lambda b, p: (b, 0, p))
    return pl.pallas_call(
        kern, grid=(batch, DIFF_PAIRS),
        in_specs=[pl.BlockSpec(memory_space=pltpu.SMEM), _full((4, DIFF_D)),
                  pl.BlockSpec((None, 2, LANES), lambda b, p: (p, 0, 0)),
                  new, cache, cache, new, new, new, _full((1, pair_w))],
        out_specs=new,
        out_shape=jax.ShapeDtypeStruct((batch * seq_len, DIFF_WIDTH), F32),
        compiler_params=_params(("parallel", "parallel")), name="diff_cache",
    )(lam_init_arr, lam_params, slopes, qd, cache_k, cache_v, kd, vd, sz, subln2)


_EXT_PAD = SUBLANES


def _lru_kernel(x_ref, sz_ref, conv0_ref, h0_ref, cw_ref, cb_ref, wax_ref, ba_ref, bx_ref, lam_ref,
                o_ref, hlast_ref, convnew_ref, ext_ref, a_ref, b_ref, hs_ref, hc_ref, *, tt, past):
    i = pl.program_id(1)
    hist = CONV_W - 1

    @pl.when(i == 0)
    def _():
        hc_ref[...] = h0_ref[...]
        ext_ref[_EXT_PAD - hist:_EXT_PAD, :] = conv0_ref[...]

    ext_ref[_EXT_PAD:_EXT_PAD + tt, :] = x_ref[...]
    cw = cw_ref[...]
    xc = cb_ref[...]
    for k in range(CONV_W):
        xc = xc + ext_ref[_EXT_PAD - hist + k:_EXT_PAD - hist + k + tt, :] * cw[k:k + 1, :]
    gates = jnp.dot(xc.astype(MXU_DTYPE), wax_ref[...], preferred_element_type=F32)
    r = jax.nn.sigmoid(gates[:, :LRU_WIDTH] + ba_ref[...])
    gi = jax.nn.sigmoid(gates[:, LRU_WIDTH:] + bx_ref[...])
    neg_lam = -lam_ref[...]
    softplus = jnp.maximum(neg_lam, 0.0) + jnp.log1p(jnp.exp(-jnp.abs(neg_lam)))
    log_a = -LRU_C * r * softplus
    a = jnp.exp(log_a)
    qpos = past + i * tt + lax.broadcasted_iota(jnp.int32, (tt, LRU_WIDTH), 0)
    mult = jnp.where(qpos == 0, 1.0, jnp.sqrt(1.0 - a * a))
    a_ref[...] = a
    b_ref[...] = mult * gi * xc

    def row(t, h):
        h = a_ref[pl.ds(t, 1), :] * h + b_ref[pl.ds(t, 1), :]
        hs_ref[pl.ds(t, 1), :] = h
        return h

    h = lax.fori_loop(0, tt, row, hc_ref[...], unroll=8)
    hc_ref[...] = h
    o_ref[...] = hs_ref[...] * sz_ref[...]
    hlast_ref[...] = h
    tail = ext_ref[_EXT_PAD + tt - hist:_EXT_PAD + tt, :]
    convnew_ref[...] = tail
    ext_ref[_EXT_PAD - hist:_EXT_PAD, :] = tail


def _lru_call(x_lru, sz, conv0, h0, lw, batch, seq_len, past):
    tt = min(seq_len, 512)
    assert seq_len % tt == 0 and tt >= CONV_W - 1
    nt = seq_len // tt
    hist = CONV_W - 1
    kern = functools.partial(_lru_kernel, tt=tt, past=past)
    tile = pl.BlockSpec((tt, LRU_WIDTH), lambda b, i: (b * nt + i, 0))
    return pl.pallas_call(
        kern, grid=(batch, nt),
        in_specs=[tile, tile,
                  pl.BlockSpec((None, hist, LRU_WIDTH), lambda b, i: (b, 0, 0)),
                  pl.BlockSpec((None, 1, LRU_WIDTH), lambda b, i: (b, 0, 0)),
                  _full((CONV_W, LRU_WIDTH)), _full((1, LRU_WIDTH)),
                  _full((LRU_WIDTH, 2 * LRU_WIDTH)), _full((1, LRU_WIDTH)), _full((1, LRU_WIDTH)),
                  _full((1, LRU_WIDTH))],
        out_specs=[tile,
                   pl.BlockSpec((None, 1, LRU_WIDTH), lambda b, i: (b, 0, 0)),
                   pl.BlockSpec((None, hist, LRU_WIDTH), lambda b, i: (b, 0, 0))],
        out_shape=[jax.ShapeDtypeStruct((batch * seq_len, LRU_WIDTH), F32),
                   jax.ShapeDtypeStruct((batch, 1, LRU_WIDTH), F32),
                   jax.ShapeDtypeStruct((batch, hist, LRU_WIDTH), F32)],
        scratch_shapes=[pltpu.VMEM((_EXT_PAD + tt, LRU_WIDTH), F32), pltpu.VMEM((tt, LRU_WIDTH), F32),
                        pltpu.VMEM((tt, LRU_WIDTH), F32), pltpu.VMEM((tt, LRU_WIDTH), F32),
                        pltpu.VMEM((1, LRU_WIDTH), F32)],
        compiler_params=_params(("parallel", "arbitrary")), name="lru",
    )(x_lru, sz, conv0, h0, lw['conv_w'], lw['conv_b'], lw['w_ax_bd'], lw['b_a'], lw['b_x'], lw['lam'])


def _merge_kernel(x_ref, g_ref, wg_ref, om_ref, od_ref, ol_ref, wom_ref, wod_ref, wol_ref, wout_ref,
                  fg_ref, o_ref, *, final):
    x = x_ref[...]
    xb = _rms(x, g_ref[...]).astype(MXU_DTYPE)
    merged = None
    for b, (br_ref, w_ref) in enumerate(((om_ref, wom_ref), (od_ref, wod_ref), (ol_ref, wol_ref))):
        gate = jax.nn.sigmoid(jnp.dot(xb, wg_ref[:, b * D_MODEL:(b + 1) * D_MODEL],
                                      preferred_element_type=F32))
        term = gate * jnp.dot(br_ref[...].astype(MXU_DTYPE), w_ref[...], preferred_element_type=F32)
        merged = term if merged is None else merged + term
    y = x + jnp.dot(merged.astype(MXU_DTYPE), wout_ref[...], preferred_element_type=F32)
    if final:
        y = _rms(y, fg_ref[...])
    o_ref[...] = y


def _merge_call(x2, o_mla, o_diff, o_lru, lw, final_g, final):
    n = x2.shape[0]
    tm = min(n, 256)
    row = lambda w: pl.BlockSpec((tm, w), lambda i: (i, 0))
    return pl.pallas_call(
        functools.partial(_merge_kernel, final=final), grid=(n // tm,),
        in_specs=[row(D_MODEL), _full((1, D_MODEL)), _full((D_MODEL, N_BRANCH * D_MODEL)),
                  row(MLA_WIDTH), row(DIFF_WIDTH), row(LRU_WIDTH),
                  _full((MLA_WIDTH, D_MODEL)), _full((DIFF_WIDTH, D_MODEL)), _full((LRU_WIDTH, D_MODEL)),
                  _full((D_MODEL, D_MODEL)), _full((1, D_MODEL))],
        out_specs=row(D_MODEL),
        out_shape=jax.ShapeDtypeStruct((n, D_MODEL), F32),
        compiler_params=_params(("parallel",)), name="merge",
    )(x2, lw['norm'], lw['w_gate'], o_mla, o_diff, o_lru, lw['w_o_mla'], lw['w_o_diff'], lw['w_o_lru'],
      lw['w_out'], final_g)


def _block_diag(blocks):
    n, r, c = blocks.shape
    eye = jnp.eye(n, dtype=blocks.dtype)
    return (blocks[:, :, None, :] * eye[:, None, :, None]).reshape(n * r, n * c)


def _swap_halves(w, groups):
    rows, cols = w.shape
    return w.reshape(rows, groups, 2, cols // groups // 2)[:, :, ::-1, :].reshape(rows, cols)


def _layer_weights(l, norm_g, w_in, mla_q_norm, mla_kv_norm, mla_w_uq, mla_w_uk, mla_w_uv, diff_subln,
                   lru_conv_w, lru_conv_b, lru_w_a, lru_b_a, lru_w_x, lru_b_x, lru_lambda, w_o_mla,
                   w_o_diff, w_o_lru, w_out):
    cast = lambda w: w.astype(MXU_DTYPE)
    names = ('c_q', 'c_kv', 'k_r', 'z_mla', 'q_d', 'k_d', 'v_d', 'z_diff', 'x_lru', 'z_lru', 'gate')
    widths = (MLA_Q_RANK, MLA_KV_RANK, MLA_ROPE, MLA_WIDTH, DIFF_WIDTH, DIFF_WIDTH, DIFF_WIDTH, DIFF_WIDTH,
              LRU_WIDTH, LRU_WIDTH, N_BRANCH * D_MODEL)
    cols, off = {}, 0
    for name, w in zip(names, widths):
        cols[name] = w_in[l][:, off:off + w]
        off += w
    uq = mla_w_uq[l].reshape(MLA_Q_RANK, MLA_HEADS, MLA_NOPE + MLA_ROPE)
    uq_rope = uq[:, :, MLA_NOPE:].reshape(MLA_Q_RANK, MLA_HEADS * MLA_ROPE)
    w_kr = cols['k_r']
    return {
        'norm': norm_g[l][None],
        'w_main': cast(jnp.concatenate([cols[n] for n, _ in _MAIN], axis=1)),
        'w_kr': cast(w_kr), 'w_kr_sw': cast(_swap_halves(w_kr, 1)),
        'w_gate': cast(cols['gate']),
        'q_norm': mla_q_norm[l][None], 'kv_norm': mla_kv_norm[l][None],
        'w_uq_nope': cast(uq[:, :, :MLA_NOPE].reshape(MLA_Q_RANK, MLA_HEADS * MLA_NOPE)),
        'w_uq_rope': cast(uq_rope), 'w_uq_rope_sw': cast(_swap_halves(uq_rope, MLA_HEADS)),
        'w_uk_bd': cast(_block_diag(mla_w_uk[l].transpose(1, 2, 0))),
        'w_uv_bd': cast(_block_diag(mla_w_uv[l].transpose(1, 0, 2))),
        'subln2': jnp.tile(diff_subln[l], 2)[None],
        'conv_w': lru_conv_w[l], 'conv_b': lru_conv_b[l][None],
        'w_ax_bd': cast(jnp.concatenate([_block_diag(lru_w_a[l]), _block_diag(lru_w_x[l])], axis=1)),
        'b_a': lru_b_a[l][None], 'b_x': lru_b_x[l][None], 'lam': lru_lambda[l][None],
        'w_o_mla': cast(w_o_mla[l]), 'w_o_diff': cast(w_o_diff[l]), 'w_o_lru': cast(w_o_lru[l]),
        'w_out': cast(w_out[l]),
    }


def _rope_tables(past, seq_len):
    half = MLA_ROPE // 2
    inv = ROPE_BASE ** (-jnp.arange(half, dtype=F32) / half)
    ang = (past + jnp.arange(seq_len, dtype=jnp.int32)).astype(F32)[:, None] * inv[None, :]
    cos, sin = jnp.cos(ang), jnp.sin(ang)
    return (jnp.tile(jnp.concatenate([cos, cos], axis=1), (1, MLA_HEADS)),
            jnp.tile(jnp.concatenate([-sin, sin], axis=1), (1, MLA_HEADS)))


ATTN_TQ = 256
ATTN_TK = 256


def _layer(x2, past_state, lw, layer_idx, lam_params, batch, seq_len, past, final_g, final):
    cache_ckv, cache_kr, cache_dk, cache_dv, h0, conv0 = past_state
    keys_on_rows = past == 0
    tq, tk = ATTN_TQ, ATTN_TK
    assert not keys_on_rows or (seq_len % tq == 0 and seq_len % tk == 0)
    cos_tab, sin_tab = _rope_tables(past, seq_len)
    ops = _proj_call(x2, lw, cos_tab, sin_tab, seq_len, keys_on_rows)

    lam_init = 0.8 - 0.6 * math.exp(-0.3 * layer_idx)
    lam_init_arr = jnp.array([lam_init], F32)
    slope = 2.0 ** (-8.0 * jnp.arange(1, DIFF_HEADS + 1, dtype=F32) / DIFF_HEADS)
    slopes = jnp.broadcast_to(slope.reshape(DIFF_PAIRS, 2, 1), (DIFF_PAIRS, 2, LANES))
    if keys_on_rows:
        o_mla = _mla_kr_call(ops['q'], ops['kcat'], ops['ckv_t'], ops['sz_mla'], lw['w_uv_bd'].T, batch, seq_len,
                             past, seq_len, tq, tk)
        o_diff = _diff_kr_call(lam_init_arr, lam_params, slopes, ops['qd'], ops['kd_rows'], ops['vd_t'],
                               ops['sz_diff'], lw['subln2'], batch, seq_len, past, seq_len, tq, tk)
    else:
        o_mla = _mla_cache_call(ops['q'], cache_ckv, cache_kr, ops['ckv'], ops['kr'], ops['sz_mla'], lw['w_uv_bd'],
                                batch, seq_len, past)
        o_diff = _diff_cache_call(lam_init_arr, lam_params, slopes, ops['qd'],
                                  cache_dk.reshape(batch, past, DIFF_WIDTH),
                                  cache_dv.reshape(batch, past, DIFF_WIDTH), ops['kd'], ops['vd'], ops['sz_diff'],
                                  lw['subln2'], batch, seq_len, past)
    ckv, kr, kd, vd = ops['ckv'], ops['kr'], ops['kd'], ops['vd']

    o_lru, h_last, conv_new = _lru_call(ops['x_lru'], ops['sz_lru'], conv0, h0[:, None, :], lw, batch, seq_len,
                                        past)

    x_new = _merge_call(x2, o_mla, o_diff, o_lru, lw, final_g, final)
    states = (ckv.reshape(batch, seq_len, MLA_KV_RANK), kr.reshape(batch, seq_len, MLA_ROPE),
              kd.reshape(batch, seq_len, DIFF_HEADS, DIFF_V), vd.reshape(batch, seq_len, DIFF_HEADS, DIFF_V),
              h_last[:, 0, :], conv_new)
    return x_new, states


def kernel(x_prompt, x_sample, cache_mla_ckv, cache_mla_krope, cache_diff_k, cache_diff_v, state_lru_h, state_lru_conv, norm_g, w_in, mla_q_norm, mla_kv_norm, mla_w_uq, mla_w_uk, mla_w_uv, diff_lq1, diff_lk1, diff_lq2, diff_lk2, diff_subln, lru_conv_w, lru_conv_b, lru_w_a, lru_b_a, lru_w_x, lru_b_x, lru_lambda, w_o_mla, w_o_diff, w_o_lru, w_out, final_norm):
    depth = w_in.shape[0]
    bp, tp, _ = x_prompt.shape
    bs, ts, _ = x_sample.shape
    past = cache_mla_ckv.shape[2]
    zeros_h = jnp.zeros((bp, LRU_WIDTH), F32)
    zeros_conv = jnp.zeros((bp, CONV_W - 1, LRU_WIDTH), F32)
    xp = x_prompt.reshape(bp * tp, D_MODEL)
    xs = x_sample.reshape(bs * ts, D_MODEL)
    final_g = final_norm[None]
    p_states, s_states = [], []
    for l in range(depth):
        lw = _layer_weights(l, norm_g, w_in, mla_q_norm, mla_kv_norm, mla_w_uq, mla_w_uk, mla_w_uv,
                            diff_subln, lru_conv_w, lru_conv_b, lru_w_a, lru_b_a, lru_w_x, lru_b_x,
                            lru_lambda, w_o_mla, w_o_diff, w_o_lru, w_out)
        lam_params = jnp.stack([diff_lq1[l], diff_lk1[l], diff_lq2[l], diff_lk2[l]])
        final = l == depth - 1
        xp, st_p = _layer(xp, (None, None, None, None, zeros_h, zeros_conv), lw, l, lam_params, bp, tp, 0,
                          final_g, final)
        xs, st_s = _layer(xs, (cache_mla_ckv[l], cache_mla_krope[l], cache_diff_k[l], cache_diff_v[l],
                               state_lru_h[l], state_lru_conv[l]), lw, l, lam_params, bs, ts, past,
                          final_g, final)
        p_states.append(st_p)
        s_states.append(st_s)
    stack = lambda states, i: jnp.stack([s[i] for s in states], axis=0)
    return (xp.reshape(bp, tp, D_MODEL), xs.reshape(bs, ts, D_MODEL),
            *(stack(p_states, i) for i in range(6)), *(stack(s_states, i) for i in range(6)))
```
